```python
import jax, jax.numpy as jnp
from jax import lax
import numpy as np

D_MODEL = 1024
BATCH = 16
SEQ = 2048
DEPTH = 1

MEM_LEN = 256
POOL_WINDOWS = (2, 4, 8, 16)
POOL_GROUPS = 4
POOL_GROUP_DIM = D_MODEL // 8
POOL_WIDTH = POOL_GROUPS * POOL_GROUP_DIM
RET_HEADS = 4
RET_QK_DIM = D_MODEL // 8
RET_V_DIM = 2 * RET_QK_DIM
RET_QK_WIDTH = RET_HEADS * RET_QK_DIM
RET_V_WIDTH = RET_HEADS * RET_V_DIM
RET_CHUNK = 128
ROPE_BASE = 10000.0
XA_HEADS = 4
XA_HEAD_DIM = D_MODEL // 8
XA_WIDTH = XA_HEADS * XA_HEAD_DIM
N_BRANCH = 3
IN_WIDTH = POOL_WIDTH + 2 * RET_QK_WIDTH + 2 * RET_V_WIDTH + XA_WIDTH + N_BRANCH * D_MODEL
FFN_HIDDEN = 2816
CONV_WIDTH = 3
EPS = 1e-6

kernel_name = "hybrid_pool_retention_memxattn_convglu"


def rmsnorm(x, g):
    x32 = x.astype(jnp.float32)
    y = x32 * lax.rsqrt(jnp.mean(x32 * x32, axis=-1, keepdims=True) + EPS)
    return y.astype(x.dtype) * g


def pool_mixer(hp, w_pool, pool_scale):
    B, S, _ = hp.shape
    h32 = hp.astype(jnp.float32)
    cs = jnp.cumsum(h32, axis=1)
    t1 = jnp.arange(1, S + 1, dtype=jnp.float32)[None, :, None]
    outs = []
    for gi, w in enumerate(POOL_WINDOWS):
        c = cs[..., gi * POOL_GROUP_DIM:(gi + 1) * POOL_GROUP_DIM]
        c_shift = jnp.pad(c, ((0, 0), (w, 0), (0, 0)))[:, :S]
        outs.append((c - c_shift) / jnp.minimum(t1, float(w)))
    pooled = jnp.concatenate(outs, axis=-1) - h32
    pooled = pooled.astype(hp.dtype).reshape(B, S, POOL_GROUPS, POOL_GROUP_DIM)
    y = jnp.einsum('bsgc,gcd->bsgd', pooled, w_pool).reshape(B, S, POOL_WIDTH)
    return y * pool_scale


def rotary(x, pos):
    half = x.shape[-1] // 2
    inv = ROPE_BASE ** (-jnp.arange(half, dtype=jnp.float32) / half)
    ang = pos[:, None] * inv[None, :]
    cos = jnp.cos(ang)[None, :, None, :]
    sin = jnp.sin(ang)[None, :, None, :]
    x1, x2 = x[..., :half], x[..., half:]
    return jnp.concatenate([x1 * cos - x2 * sin, x1 * sin + x2 * cos], axis=-1)


def chunkwise_retention(q, k, v):
    B, S, H, dk = q.shape
    dv = v.shape[-1]
    C = RET_CHUNK
    N = S // C
    log_gamma = jnp.log1p(-jnp.exp2(-5.0 - jnp.arange(H, dtype=jnp.float32)))
    lg = log_gamma[:, None, None]
    idx = jnp.arange(C, dtype=jnp.float32)
    rel = idx[:, None] - idx[None, :]
    decay_intra = jnp.where(rel >= 0, jnp.exp(jnp.maximum(rel, 0.0) * lg), 0.0)
    q_decay = jnp.exp((idx + 1.0)[None, :, None] * lg)
    k_decay = jnp.exp((C - 1.0 - idx)[None, :, None] * lg)
    chunk_decay = jnp.exp(C * lg)

    def to_chunks(a):
        d = a.shape[-1]
        return a.reshape(B, N, C, H, d).transpose(1, 0, 3, 2, 4)

    qc, kc, vc = to_chunks(q), to_chunks(k * (dk ** -0.5)), to_chunks(v)

    def step(R, inp):
        qi, ki, vi = inp
        s = jnp.einsum('bhqd,bhkd->bhqk', qi, ki) * decay_intra
        o = (jnp.einsum('bhqk,bhkv->bhqv', s, vi)
             + jnp.einsum('bhqd,bhdv->bhqv', qi * q_decay, R))
        R = chunk_decay * R + jnp.einsum('bhkd,bhkv->bhdv', ki * k_decay, vi)
        return R, o

    R0 = jnp.zeros((B, H, dk, dv), jnp.float32)
    _, o = lax.scan(step, R0, (qc, kc, vc))
    return o.transpose(1, 0, 3, 2, 4).reshape(B, S, H, dv)


def retention_branch(q, k, v, gr, g_ret, b_ret):
    B, S, _ = q.shape
    pos = jnp.arange(S, dtype=jnp.float32)
    q4 = rotary(q.astype(jnp.float32).reshape(B, S, RET_HEADS, RET_QK_DIM), pos)
    k4 = rotary(k.astype(jnp.float32).reshape(B, S, RET_HEADS, RET_QK_DIM), pos)
    v4 = v.astype(jnp.float32).reshape(B, S, RET_HEADS, RET_V_DIM)
    o = chunkwise_retention(q4, k4, v4)
    mu = jnp.mean(o, axis=-1, keepdims=True)
    var = jnp.mean(jnp.square(o - mu), axis=-1, keepdims=True)
    o = ((o - mu) * lax.rsqrt(var + EPS)).reshape(B, S, RET_V_WIDTH).astype(q.dtype)
    o = o * g_ret + b_ret
    return jax.nn.silu(gr) * o


def memory_cross_attention(qx, mem_n, w_mem_kv):
    B, S, _ = qx.shape
    M = mem_n.shape[1]
    q = qx.reshape(B, S, XA_HEADS, XA_HEAD_DIM)
    kv = mem_n @ w_mem_kv
    k, v = jnp.split(kv, 2, axis=-1)
    k = k.reshape(B, M, XA_HEADS, XA_HEAD_DIM)
    v = v.reshape(B, M, XA_HEADS, XA_HEAD_DIM)
    s = jnp.einsum('bshd,bmhd->bhsm', q, k).astype(jnp.float32) * (XA_HEAD_DIM ** -0.5)
    p = jax.nn.softmax(s, axis=-1).astype(v.dtype)
    o = jnp.einsum('bhsm,bmhd->bshd', p, v)
    return o.reshape(B, S, XA_WIDTH)


def conv_glu_ffn(h, w_up, conv_w, conv_b, w_down):
    S = h.shape[1]
    up = h @ w_up
    a, b = jnp.split(up, 2, axis=-1)
    a_pad = jnp.pad(a, ((0, 0), (CONV_WIDTH - 1, 0), (0, 0)))
    a = sum(a_pad[:, j:j + S] * conv_w[j] for j in range(CONV_WIDTH)) + conv_b
    return (jax.nn.gelu(a) * b) @ w_down


def setup_inputs(seed: int = 0) -> dict:
    key = jax.random.key(seed)
    ks = jax.random.split(key, 24)
    f32 = jnp.float32

    def nrm(k, shape, fan_in):
        return jax.random.normal(k, shape, f32) * (fan_in ** -0.5)

    def gain(k, shape):
        return 1.0 + 0.02 * jax.random.normal(k, shape, f32)

    L = DEPTH
    return {
        "x": jax.random.normal(ks[0], (BATCH, SEQ, D_MODEL), f32),
        "mem": jax.random.normal(ks[1], (BATCH, MEM_LEN, D_MODEL), f32),
        "g_mix": gain(ks[2], (L, D_MODEL)),
        "w_in": nrm(ks[3], (L, D_MODEL, IN_WIDTH), D_MODEL),
        "w_pool": nrm(ks[4], (L, POOL_GROUPS, POOL_GROUP_DIM, POOL_GROUP_DIM), POOL_GROUP_DIM),
        "pool_scale": 1.0 + 0.1 * jax.random.normal(ks[5], (L, POOL_WIDTH), f32),
        "w_a": nrm(ks[6], (L, POOL_WIDTH, D_MODEL), POOL_WIDTH),
        "g_ret": gain(ks[7], (L, RET_V_WIDTH)),
        "b_ret": 0.01 * jax.random.normal(ks[8], (L, RET_V_WIDTH), f32),
        "w_r": nrm(ks[9], (L, RET_V_WIDTH, D_MODEL), RET_V_WIDTH),
        "g_mem": gain(ks[10], (L, D_MODEL)),
        "w_mem_kv": nrm(ks[11], (L, D_MODEL, 2 * XA_WIDTH), D_MODEL),
        "w_c": nrm(ks[12], (L, XA_WIDTH, D_MODEL), XA_WIDTH),
        "w_out": nrm(ks[13], (L, D_MODEL, D_MODEL), D_MODEL),
        "g_ffn": gain(ks[14], (L, D_MODEL)),
        "w_up": nrm(ks[15], (L, D_MODEL, 2 * FFN_HIDDEN), D_MODEL),
        "conv_w": nrm(ks[16], (L, CONV_WIDTH, FFN_HIDDEN), CONV_WIDTH),
        "conv_b": 0.01 * jax.random.normal(ks[17], (L, FFN_HIDDEN), f32),
        "w_down": nrm(ks[18], (L, FFN_HIDDEN, D_MODEL), FFN_HIDDEN),
        "g_final": gain(ks[19], (D_MODEL,)),
    }


def reference(x, mem, g_mix, w_in, w_pool, pool_scale, w_a, g_ret, b_ret, w_r,
              g_mem, w_mem_kv, w_c, w_out, g_ffn, w_up, conv_w, conv_b, w_down, g_final):
    splits = list(np.cumsum([POOL_WIDTH, RET_QK_WIDTH, RET_QK_WIDTH, RET_V_WIDTH,
                             RET_V_WIDTH, XA_WIDTH]))
    for l in range(DEPTH):
        h = rmsnorm(x, g_mix[l])
        proj = h @ w_in[l]
        hp, q, k, v, gr, qx, gl = jnp.split(proj, splits, axis=-1)
        y_pool = pool_mixer(hp, w_pool[l], pool_scale[l]) @ w_a[l]
        y_ret = retention_branch(q, k, v, gr, g_ret[l], b_ret[l]) @ w_r[l]
        mem_n = rmsnorm(mem, g_mem[l])
        y_mem = memory_cross_attention(qx, mem_n, w_mem_kv[l]) @ w_c[l]
        gate_pool, gate_ret, gate_mem = jnp.split(gl, N_BRANCH, axis=-1)
        merged = (jax.nn.sigmoid(gate_pool) * y_pool
                  + jax.nn.sigmoid(gate_ret) * y_ret
                  + jax.nn.sigmoid(gate_mem) * y_mem)
        x = x + merged @ w_out[l]
        x = x + conv_glu_ffn(rmsnorm(x, g_ffn[l]), w_up[l], conv_w[l], conv_b[l], w_down[l])
    return rmsnorm(x, g_final)
```

```python
import functools

import jax
import jax.numpy as jnp
import numpy as np
from jax import lax
from jax.experimental import pallas as pl
from jax.experimental.pallas import tpu as pltpu

D_MODEL = 1024
POOL_WINDOWS = (2, 4, 8, 16)
POOL_GROUP_DIM = D_MODEL // 8
POOL_WIDTH = len(POOL_WINDOWS) * POOL_GROUP_DIM
POOL_HALO = 16
RET_HEADS = 4
RET_QK_DIM = D_MODEL // 8
RET_V_DIM = 2 * RET_QK_DIM
RET_QK_WIDTH = RET_HEADS * RET_QK_DIM
RET_V_WIDTH = RET_HEADS * RET_V_DIM
RET_CHUNK = 128
ROPE_BASE = 10000.0
XA_HEADS = 4
XA_HEAD_DIM = D_MODEL // 8
XA_WIDTH = XA_HEADS * XA_HEAD_DIM
FFN_HIDDEN = 2816
CONV_WIDTH = 3
CONV_HALO = 8
EPS = 1e-6

OFF_HP = 0
OFF_Q = OFF_HP + POOL_WIDTH
OFF_K = OFF_Q + RET_QK_WIDTH
OFF_V = OFF_K + RET_QK_WIDTH
OFF_GR = OFF_V + RET_V_WIDTH
OFF_QX = OFF_GR + RET_V_WIDTH
OFF_GATE = OFF_QX + XA_WIDTH

MIXER_ROWS = 256
FFN_ROWS = 512
FFN_COLS = 256
V7X_VMEM_LIMIT_BYTES = 56 * 1024 * 1024

BF16 = jnp.bfloat16
F32 = jnp.float32


def _dot(a, b):
    return jnp.dot(a, b, preferred_element_type=F32)


def _rms(x):
    return x * lax.rsqrt(jnp.mean(x * x, axis=-1, keepdims=True) + EPS)


def _mixer_kernel(x_ref, mem_ref, cos_ref, sin_ref, dintra_ref, qdec_ref, kdec_ref, cdec_ref,
                  g_mix_ref, w_in_ref, w_pool_ref, pool_scale_ref, w_a_ref, g_ret_ref, b_ret_ref,
                  w_r_ref, g_mem_ref, w_mem_kv_ref, w_c_ref, w_out_ref,
                  o_ref,
                  h_scr, e_scr, r_scr, kt_scr, v_scr, ret_scr, m_scr):
    ts = x_ref.shape[0]
    s_idx = pl.program_id(1)

    @pl.when(s_idx == 0)
    def _start_of_sequence():
        e_scr[0:POOL_HALO, :] = jnp.zeros((POOL_HALO, POOL_WIDTH), F32)
        r_scr[...] = jnp.zeros(r_scr.shape, F32)
        mem_n = (_rms(mem_ref[...]) * g_mem_ref[...]).astype(BF16)
        kv = _dot(mem_n, w_mem_kv_ref[...])
        kt_scr[...] = kv[:, :XA_WIDTH].T.astype(BF16)
        v_scr[...] = kv[:, XA_WIDTH:].astype(BF16)

    x = x_ref[...]
    h_scr[...] = (_rms(x) * g_mix_ref[...]).astype(BF16)

    def proj(off, width):
        return _dot(h_scr[...], w_in_ref[:, off:off + width])

    def gate(i):
        return jax.nn.sigmoid(proj(OFF_GATE + i * D_MODEL, D_MODEL))

    hp = proj(OFF_HP, POOL_WIDTH)
    e_scr[POOL_HALO:POOL_HALO + ts, :] = hp
    t1 = (lax.broadcasted_iota(jnp.int32, (ts, POOL_GROUP_DIM), 0) + (s_idx * ts + 1)).astype(F32)
    pooled = []
    for gi, w in enumerate(POOL_WINDOWS):
        c0 = gi * POOL_GROUP_DIM
        cur = hp[:, c0:c0 + POOL_GROUP_DIM]
        acc = cur
        for j in range(1, w):
            acc = acc + e_scr[POOL_HALO - j:POOL_HALO - j + ts, c0:c0 + POOL_GROUP_DIM]
        pooled.append(acc / jnp.minimum(t1, float(w)) - cur)
    e_scr[0:POOL_HALO, :] = hp[ts - POOL_HALO:, :]
    pooled = jnp.concatenate(pooled, axis=-1).astype(BF16)
    y = _dot(pooled, w_pool_ref[...]) * pool_scale_ref[...]
    y = _dot(y.astype(BF16), w_a_ref[...])
    m_scr[...] = gate(0) * y

    cos = cos_ref[...]
    sin = sin_ref[...]

    def rope(a):
        parts = []
        for hh in range(RET_HEADS):
            ah = a[:, hh * RET_QK_DIM:(hh + 1) * RET_QK_DIM]
            parts.append(ah * cos + pltpu.roll(ah, RET_QK_DIM // 2, 1) * sin)
        return jnp.concatenate(parts, axis=-1)

    q = rope(proj(OFF_Q, RET_QK_WIDTH))
    k = rope(proj(OFF_K, RET_QK_WIDTH)) * (RET_QK_DIM ** -0.5)
    v = proj(OFF_V, RET_V_WIDTH).astype(BF16)
    for c in range(ts // RET_CHUNK):
        r0 = c * RET_CHUNK
        qc = q[r0:r0 + RET_CHUNK, :]
        kc = k[r0:r0 + RET_CHUNK, :]
        qd = (qc * qdec_ref[...]).astype(BF16)
        kd = kc * kdec_ref[...]
        qc = qc.astype(BF16)
        kc = kc.astype(BF16)
        for hh in range(RET_HEADS):
            qs = slice(hh * RET_QK_DIM, (hh + 1) * RET_QK_DIM)
            vs = slice(hh * RET_V_DIM, (hh + 1) * RET_V_DIM)
            vh = v[r0:r0 + RET_CHUNK, vs]
            sc = lax.dot_general(qc[:, qs], kc[:, qs], (((1,), (1,)), ((), ())),
                                 preferred_element_type=F32)
            sc = (sc * dintra_ref[hh]).astype(BF16)
            state = r_scr[:, vs]
            ret_scr[r0:r0 + RET_CHUNK, vs] = _dot(sc, vh) + _dot(qd[:, qs], state.astype(BF16))
            r_scr[:, vs] = cdec_ref[:, vs] * state + _dot(kd[:, qs].T.astype(BF16), vh)
    gr = proj(OFF_GR, RET_V_WIDTH)
    normed = []
    for hh in range(RET_HEADS):
        oh = ret_scr[:, hh * RET_V_DIM:(hh + 1) * RET_V_DIM]
        dev = oh - jnp.mean(oh, axis=-1, keepdims=True)
        normed.append(dev * lax.rsqrt(jnp.mean(dev * dev, axis=-1, keepdims=True) + EPS))
    o = jnp.concatenate(normed, axis=-1) * g_ret_ref[...] + b_ret_ref[...]
    y = _dot((jax.nn.silu(gr) * o).astype(BF16), w_r_ref[...])
    m_scr[...] += gate(1) * y

    qx = proj(OFF_QX, XA_WIDTH).astype(BF16)
    heads = []
    for hh in range(XA_HEADS):
        hs = slice(hh * XA_HEAD_DIM, (hh + 1) * XA_HEAD_DIM)
        sc = _dot(qx[:, hs], kt_scr[hs, :]) * (XA_HEAD_DIM ** -0.5)
        p = jnp.exp(sc - jnp.max(sc, axis=-1, keepdims=True))
        heads.append(_dot(p.astype(BF16), v_scr[:, hs]) / jnp.sum(p, axis=-1, keepdims=True))
    y = _dot(jnp.concatenate(heads, axis=-1).astype(BF16), w_c_ref[...])
    merged = m_scr[...] + gate(2) * y

    o_ref[...] = x + _dot(merged.astype(BF16), w_out_ref[...])


def _retention_tables(seq):
    half = RET_QK_DIM // 2
    pos = jnp.arange(seq, dtype=F32)
    inv = ROPE_BASE ** (-jnp.arange(half, dtype=F32) / half)
    ang = pos[:, None] * inv[None, :]
    cos, sin = jnp.cos(ang), jnp.sin(ang)
    cos2 = jnp.concatenate([cos, cos], axis=-1)
    sin2 = jnp.concatenate([-sin, sin], axis=-1)
    c = RET_CHUNK
    log_gamma = jnp.log1p(-jnp.exp2(-5.0 - jnp.arange(RET_HEADS, dtype=F32)))
    lg = log_gamma[:, None, None]
    idx = jnp.arange(c, dtype=F32)
    rel = idx[:, None] - idx[None, :]
    decay_intra = jnp.where(rel >= 0, jnp.exp(jnp.maximum(rel, 0.0) * lg), 0.0)
    q_decay = jnp.exp((idx + 1.0)[None, :, None] * lg)
    k_decay = jnp.exp((c - 1.0 - idx)[None, :, None] * lg)
    chunk_decay = jnp.exp(c * lg)

    def lanes(a, width):
        r = a.shape[1]
        return jnp.broadcast_to(a.transpose(1, 0, 2), (r, RET_HEADS, width)).reshape(r, RET_HEADS * width)

    return (cos2, sin2, decay_intra, lanes(q_decay, RET_QK_DIM), lanes(k_decay, RET_QK_DIM),
            lanes(chunk_decay, RET_V_DIM))


def _vmem_full():
    return pl.BlockSpec(memory_space=pltpu.VMEM)


def _mixer(x, mem, tables, g_mix, w_in, w_pool_bd, pool_scale, w_a, g_ret, b_ret, w_r, g_mem,
           w_mem_kv, w_c, w_out):
    b, s, d = x.shape
    m = mem.shape[1]
    ts = min(MIXER_ROWS, s)
    assert s % ts == 0 and ts % RET_CHUNK == 0
    cos2, sin2, dintra, qdec, kdec, cdec = tables
    row_tile = pl.BlockSpec((None, ts, d), lambda bi, si: (bi, si, 0))
    pos_tile = pl.BlockSpec((ts, RET_QK_DIM), lambda bi, si: (si, 0))
    in_specs = [row_tile,
                pl.BlockSpec((None, m, d), lambda bi, si: (bi, 0, 0)),
                pos_tile, pos_tile] + [_vmem_full()] * 16
    return pl.pallas_call(
        _mixer_kernel,
        grid=(b, s // ts),
        in_specs=in_specs,
        out_specs=row_tile,
        out_shape=jax.ShapeDtypeStruct((b, s, d), F32),
        scratch_shapes=[
            pltpu.VMEM((ts, d), BF16),
            pltpu.VMEM((ts + POOL_HALO, POOL_WIDTH), F32),
            pltpu.VMEM((RET_QK_DIM, RET_V_WIDTH), F32),
            pltpu.VMEM((XA_WIDTH, m), BF16),
            pltpu.VMEM((m, XA_WIDTH), BF16),
            pltpu.VMEM((ts, RET_V_WIDTH), F32),
            pltpu.VMEM((ts, d), F32),
        ],
        compiler_params=pltpu.CompilerParams(
            dimension_semantics=("arbitrary", "arbitrary"),
            vmem_limit_bytes=V7X_VMEM_LIMIT_BYTES),
        name="mixer",
    )(x, mem, cos2, sin2, dintra, qdec, kdec, cdec,
      g_mix, w_in, w_pool_bd, pool_scale, w_a, g_ret, b_ret, w_r, g_mem, w_mem_kv, w_c, w_out)


def _ffn_kernel(x_ref, g_ffn_ref, w_up_ref, conv_w_ref, conv_b_ref, w_down_ref, g_out_ref,
                o_ref, h_scr, a_scr, carry_scr, g_scr, *, final_norm):
    ts = x_ref.shape[0]

    @pl.when(pl.program_id(1) == 0)
    def _start_of_sequence():
        carry_scr[...] = jnp.zeros(carry_scr.shape, F32)

    x = x_ref[...]
    h_scr[...] = (_rms(x) * g_ffn_ref[...]).astype(BF16)
    for j in range(FFN_HIDDEN // FFN_COLS):
        cs = slice(j * FFN_COLS, (j + 1) * FFN_COLS)
        a = _dot(h_scr[...], w_up_ref[:, cs])
        gate = _dot(h_scr[...], w_up_ref[:, FFN_HIDDEN + j * FFN_COLS:FFN_HIDDEN + (j + 1) * FFN_COLS])
        buf = a_scr.at[j % 2]
        buf[0:CONV_HALO, :] = carry_scr[:, cs]
        buf[CONV_HALO:CONV_HALO + ts, :] = a
        carry_scr[:, cs] = a[ts - CONV_HALO:, :]
        conv = a * conv_w_ref[CONV_WIDTH - 1:CONV_WIDTH, cs] + conv_b_ref[:, cs]
        for tap in range(CONV_WIDTH - 1):
            back = CONV_WIDTH - 1 - tap
            conv = conv + buf[CONV_HALO - back:CONV_HALO - back + ts, :] * conv_w_ref[tap:tap + 1, cs]
        g_scr[:, cs] = (jax.nn.gelu(conv) * gate).astype(BF16)
    y = x + _dot(g_scr[...], w_down_ref[...])
    if final_norm:
        y = _rms(y) * g_out_ref[...]
    o_ref[...] = y


def _ffn(x, g_ffn, w_up, conv_w, conv_b, w_down, g_out, final_norm):
    b, s, d = x.shape
    ts = min(FFN_ROWS, s)
    assert s % ts == 0 and FFN_HIDDEN % FFN_COLS == 0
    row_tile = pl.BlockSpec((None, ts, d), lambda bi, si: (bi, si, 0))
    return pl.pallas_call(
        functools.partial(_ffn_kernel, final_norm=final_norm),
        grid=(b, s // ts),
        in_specs=[row_tile] + [_vmem_full()] * 6,
        out_specs=row_tile,
        out_shape=jax.ShapeDtypeStruct((b, s, d), F32),
        scratch_shapes=[
            pltpu.VMEM((ts, d), BF16),
            pltpu.VMEM((2, ts + CONV_HALO, FFN_COLS), F32),
            pltpu.VMEM((CONV_HALO, FFN_HIDDEN), F32),
            pltpu.VMEM((ts, FFN_HIDDEN), BF16),
        ],
        compiler_params=pltpu.CompilerParams(
            dimension_semantics=("arbitrary", "arbitrary"),
            vmem_limit_bytes=V7X_VMEM_LIMIT_BYTES),
        name="ffn",
    )(x, g_ffn, w_up, conv_w, conv_b, w_down, g_out)


def _block_diag(w):
    g, c, dd = w.shape
    out = jnp.zeros((g * c, g * dd), w.dtype)
    for i in range(g):
        out = out.at[i * c:(i + 1) * c, i * dd:(i + 1) * dd].set(w[i])
    return out


def kernel(x, mem, g_mix, w_in, w_pool, pool_scale, w_a, g_ret, b_ret, w_r, g_mem, w_mem_kv, w_c,
           w_out, g_ffn, w_up, conv_w, conv_b, w_down, g_final):
    depth = w_in.shape[0]
    tables = _retention_tables(x.shape[1])
    row = lambda a: a.reshape(1, -1)
    for l in range(depth):
        x = _mixer(x, mem, tables, row(g_mix[l]), w_in[l].astype(BF16),
                   _block_diag(w_pool[l]).astype(BF16), row(pool_scale[l]), w_a[l].astype(BF16),
                   row(g_ret[l]), row(b_ret[l]), w_r[l].astype(BF16), row(g_mem[l]),
                   w_mem_kv[l].astype(BF16), w_c[l].astype(BF16), w_out[l].astype(BF16))
        x = _ffn(x, row(g_ffn[l]), w_up[l].astype(BF16), conv_w[l], row(conv_b[l]),
                 w_down[l].astype(BF16), row(g_final), final_norm=(l == depth - 1))
    return x
```

```python
import functools

import jax
import jax.numpy as jnp
import numpy as np
from jax import lax
from jax.experimental import pallas as pl
from jax.experimental.pallas import tpu as pltpu

D_MODEL = 1024
POOL_WINDOWS = (2, 4, 8, 16)
POOL_GROUP_DIM = D_MODEL // 8
POOL_WIDTH = len(POOL_WINDOWS) * POOL_GROUP_DIM
POOL_HALO = 16
RET_HEADS = 4
RET_QK_DIM = D_MODEL // 8
RET_V_DIM = 2 * RET_QK_DIM
RET_QK_WIDTH = RET_HEADS * RET_QK_DIM
RET_V_WIDTH = RET_HEADS * RET_V_DIM
RET_CHUNK = 128
ROPE_BASE = 10000.0
XA_HEADS = 4
XA_HEAD_DIM = D_MODEL // 8
XA_WIDTH = XA_HEADS * XA_HEAD_DIM
FFN_HIDDEN = 2816
CONV_WIDTH = 3
CONV_HALO = 8
EPS = 1e-6

OFF_HP = 0
OFF_Q = OFF_HP + POOL_WIDTH
OFF_K = OFF_Q + RET_QK_WIDTH
OFF_V = OFF_K + RET_QK_WIDTH
OFF_GR = OFF_V + RET_V_WIDTH
OFF_QX = OFF_GR + RET_V_WIDTH
OFF_GATE = OFF_QX + XA_WIDTH

MIXER_ROWS = 256
FFN_ROWS = 512
FFN_COLS = 256
V7X_VMEM_LIMIT_BYTES = 56 * 1024 * 1024

BF16 = jnp.bfloat16
F32 = jnp.float32


def _dot(a, b):
    return jnp.dot(a, b, preferred_element_type=F32)


def _rms(x):
    return x * lax.rsqrt(jnp.mean(x * x, axis=-1, keepdims=True) + EPS)


def _mixer_kernel(x_ref, mem_ref, cos_ref, sin_ref, dintra_ref, qdec_ref, kdec_ref, cdec_ref,
                  g_mix_ref, w_in_ref, w_pool_ref, pool_scale_ref, w_a_ref, g_ret_ref, b_ret_ref,
                  w_r_ref, g_mem_ref, w_mem_kv_ref, w_c_ref, w_out_ref,
                  o_ref,
                  h_scr, e_scr, r_scr, kt_scr, v_scr, ret_scr, m_scr):
    ts = x_ref.shape[0]
    s_idx = pl.program_id(1)

    @pl.when(s_idx == 0)
    def _start_of_sequence():
        e_scr[...] = jnp.zeros(e_scr.shape, F32)
        r_scr[...] = jnp.zeros(r_scr.shape, F32)
        mem_n = (_rms(mem_ref[...]) * g_mem_ref[...]).astype(BF16)
        kv = _dot(mem_n, w_mem_kv_ref[...])
        kt_scr[...] = kv[:, :XA_WIDTH].T.astype(BF16)
        v_scr[...] = kv[:, XA_WIDTH:].astype(BF16)

    x = x_ref[...]
    h_scr[...] = (_rms(x) * g_mix_ref[...]).astype(BF16)

    def proj(off, width):
        return _dot(h_scr[...], w_in_ref[:, off:off + width])

    def rope(a):
        cos = cos_ref[...]
        sin = sin_ref[...]
        parts = []
        for hh in range(RET_HEADS):
            ah = a[:, hh * RET_QK_DIM:(hh + 1) * RET_QK_DIM]
            parts.append(ah * cos + pltpu.roll(ah, RET_QK_DIM // 2, 1) * sin)
        return jnp.concatenate(parts, axis=-1)

    def qslice(hh):
        return slice(hh * RET_QK_DIM, (hh + 1) * RET_QK_DIM)

    def vslice(hh):
        return slice(hh * RET_V_DIM, (hh + 1) * RET_V_DIM)

    def retention_chunk(c, q, qd, k, kdt, v):
        rows = slice(c * RET_CHUNK, (c + 1) * RET_CHUNK)
        sc = [lax.dot_general(q[rows, qslice(hh)], k[rows, qslice(hh)], (((1,), (1,)), ((), ())),
                              preferred_element_type=F32) for hh in range(RET_HEADS)]
        state = [r_scr[:, vslice(hh)] for hh in range(RET_HEADS)]
        cross = [_dot(qd[rows, qslice(hh)], state[hh].astype(BF16)) for hh in range(RET_HEADS)]
        upd = [_dot(kdt[c][hh], v[rows, vslice(hh)]) for hh in range(RET_HEADS)]
        for hh in range(RET_HEADS):
            s_h = (sc[hh] * dintra_ref[hh]).astype(BF16)
            ret_scr[rows, vslice(hh)] = _dot(s_h, v[rows, vslice(hh)]) + cross[hh]
            r_scr[:, vslice(hh)] = cdec_ref[:, vslice(hh)] * state[hh] + upd[hh]

    n_chunks = ts // RET_CHUNK

    hp = proj(OFF_HP, POOL_WIDTH)
    q = proj(OFF_Q, RET_QK_WIDTH)
    k = proj(OFF_K, RET_QK_WIDTH)

    t1 = (lax.broadcasted_iota(jnp.int32, (ts, POOL_GROUP_DIM), 0) + (s_idx * ts + 1)).astype(F32)
    pooled = []
    for gi, w in enumerate(POOL_WINDOWS):
        cols = slice(gi * POOL_GROUP_DIM, (gi + 1) * POOL_GROUP_DIM)
        cur = hp[:, cols]
        acc = jnp.concatenate([e_scr[:, cols], cur], axis=0)
        span = 1
        while span < w:
            acc = acc + pltpu.roll(acc, span, 0)
            span *= 2
        pooled.append(acc[POOL_HALO:, :] / jnp.minimum(t1, float(w)) - cur)
    e_scr[...] = hp[ts - POOL_HALO:, :]
    pooled = jnp.concatenate(pooled, axis=-1).astype(BF16)

    v = proj(OFF_V, RET_V_WIDTH).astype(BF16)

    q = rope(q)
    k = rope(k) * (RET_QK_DIM ** -0.5)
    qd, kdt = [], []
    for c in range(n_chunks):
        rows = slice(c * RET_CHUNK, (c + 1) * RET_CHUNK)
        qd.append((q[rows, :] * qdec_ref[...]).astype(BF16))
        kd = k[rows, :] * kdec_ref[...]
        kdt.append([kd[:, qslice(hh)].T.astype(BF16) for hh in range(RET_HEADS)])
    qd = jnp.concatenate(qd, axis=0)
    q = q.astype(BF16)
    k = k.astype(BF16)

    y_pool = _dot(pooled, w_pool_ref[...]) * pool_scale_ref[...]
    qx = proj(OFF_QX, XA_WIDTH).astype(BF16)
    y_pool = _dot(y_pool.astype(BF16), w_a_ref[...])

    retention_chunk(0, q, qd, k, kdt, v)
    m_scr[...] = jax.nn.sigmoid(proj(OFF_GATE, D_MODEL)) * y_pool

    probs, denom = [], []
    for hh in range(XA_HEADS):
        hs = qslice(hh)
        sc = _dot(qx[:, hs], kt_scr[hs, :]) * (XA_HEAD_DIM ** -0.5)
        p = jnp.exp(sc - jnp.max(sc, axis=-1, keepdims=True))
        probs.append(p.astype(BF16))
        denom.append(jnp.sum(p, axis=-1, keepdims=True))

    gr = proj(OFF_GR, RET_V_WIDTH)
    for c in range(1, n_chunks):
        retention_chunk(c, q, qd, k, kdt, v)

    heads = [_dot(probs[hh], v_scr[:, qslice(hh)]) / denom[hh] for hh in range(XA_HEADS)]
    gate_mem = jax.nn.sigmoid(proj(OFF_GATE + 2 * D_MODEL, D_MODEL))
    y_mem = _dot(jnp.concatenate(heads, axis=-1).astype(BF16), w_c_ref[...])
    m_scr[...] += gate_mem * y_mem

    normed = []
    for hh in range(RET_HEADS):
        oh = ret_scr[:, vslice(hh)]
        dev = oh - jnp.mean(oh, axis=-1, keepdims=True)
        normed.append(dev * lax.rsqrt(jnp.mean(dev * dev, axis=-1, keepdims=True) + EPS))
    o = jnp.concatenate(normed, axis=-1) * g_ret_ref[...] + b_ret_ref[...]
    gate_ret = jax.nn.sigmoid(proj(OFF_GATE + D_MODEL, D_MODEL))
    y_ret = _dot((jax.nn.silu(gr) * o).astype(BF16), w_r_ref[...])
    merged = m_scr[...] + gate_ret * y_ret

    o_ref[...] = x + _dot(merged.astype(BF16), w_out_ref[...])


def _retention_tables(seq):
    half = RET_QK_DIM // 2
    pos = jnp.arange(seq, dtype=F32)
    inv = ROPE_BASE ** (-jnp.arange(half, dtype=F32) / half)
    ang = pos[:, None] * inv[None, :]
    cos, sin = jnp.cos(ang), jnp.sin(ang)
    cos2 = jnp.concatenate([cos, cos], axis=-1)
    sin2 = jnp.concatenate([-sin, sin], axis=-1)
    c = RET_CHUNK
    log_gamma = jnp.log1p(-jnp.exp2(-5.0 - jnp.arange(RET_HEADS, dtype=F32)))
    lg = log_gamma[:, None, None]
    idx = jnp.arange(c, dtype=F32)
    rel = idx[:, None] - idx[None, :]
    decay_intra = jnp.where(rel >= 0, jnp.exp(jnp.maximum(rel, 0.0) * lg), 0.0)
    q_decay = jnp.exp((idx + 1.0)[None, :, None] * lg)
    k_decay = jnp.exp((c - 1.0 - idx)[None, :, None] * lg)
    chunk_decay = jnp.exp(c * lg)

    def lanes(a, width):
        r = a.shape[1]
        return jnp.broadcast_to(a.transpose(1, 0, 2), (r, RET_HEADS, width)).reshape(r, RET_HEADS * width)

    return (cos2, sin2, decay_intra, lanes(q_decay, RET_QK_DIM), lanes(k_decay, RET_QK_DIM),
            lanes(chunk_decay, RET_V_DIM))


def _vmem_full():
    return pl.BlockSpec(memory_space=pltpu.VMEM)


def _mixer(x, mem, tables, g_mix, w_in, w_pool_bd, pool_scale, w_a, g_ret, b_ret, w_r, g_mem,
           w_mem_kv, w_c, w_out):
    b, s, d = x.shape
    m = mem.shape[1]
    ts = min(MIXER_ROWS, s)
    assert s % ts == 0 and ts % RET_CHUNK == 0
    cos2, sin2, dintra, qdec, kdec, cdec = tables
    row_tile = pl.BlockSpec((None, ts, d), lambda bi, si: (bi, si, 0))
    pos_tile = pl.BlockSpec((ts, RET_QK_DIM), lambda bi, si: (si, 0))
    in_specs = [row_tile,
                pl.BlockSpec((None, m, d), lambda bi, si: (bi, 0, 0)),
                pos_tile, pos_tile] + [_vmem_full()] * 16
    return pl.pallas_call(
        _mixer_kernel,
        grid=(b, s // ts),
        in_specs=in_specs,
        out_specs=row_tile,
        out_shape=jax.ShapeDtypeStruct((b, s, d), F32),
        scratch_shapes=[
            pltpu.VMEM((ts, d), BF16),
            pltpu.VMEM((POOL_HALO, POOL_WIDTH), F32),
            pltpu.VMEM((RET_QK_DIM, RET_V_WIDTH), F32),
            pltpu.VMEM((XA_WIDTH, m), BF16),
            pltpu.VMEM((m, XA_WIDTH), BF16),
            pltpu.VMEM((ts, RET_V_WIDTH), F32),
            pltpu.VMEM((ts, d), F32),
        ],
        compiler_params=pltpu.CompilerParams(
            dimension_semantics=("arbitrary", "arbitrary"),
            vmem_limit_bytes=V7X_VMEM_LIMIT_BYTES),
        name="mixer",
    )(x, mem, cos2, sin2, dintra, qdec, kdec, cdec,
      g_mix, w_in, w_pool_bd, pool_scale, w_a, g_ret, b_ret, w_r, g_mem, w_mem_kv, w_c, w_out)


def _ffn_kernel(x_ref, g_ffn_ref, w_up_ref, conv_w_ref, conv_b_ref, w_down_ref, g_out_ref,
                o_ref, h_scr, a_scr, carry_scr, g_scr, *, final_norm):
    ts = x_ref.shape[0]

    @pl.when(pl.program_id(1) == 0)
    def _start_of_sequence():
        carry_scr[...] = jnp.zeros(carry_scr.shape, F32)

    x = x_ref[...]
    h_scr[...] = (_rms(x) * g_ffn_ref[...]).astype(BF16)
    for j in range(FFN_HIDDEN // FFN_COLS):
        cs = slice(j * FFN_COLS, (j + 1) * FFN_COLS)
        a = _dot(h_scr[...], w_up_ref[:, cs])
        gate = _dot(h_scr[...], w_up_ref[:, FFN_HIDDEN + j * FFN_COLS:FFN_HIDDEN + (j + 1) * FFN_COLS])
        buf = a_scr.at[j % 2]
        buf[0:CONV_HALO, :] = carry_scr[:, cs]
        buf[CONV_HALO:CONV_HALO + ts, :] = a
        carry_scr[:, cs] = a[ts - CONV_HALO:, :]
        conv = a * conv_w_ref[CONV_WIDTH - 1:CONV_WIDTH, cs] + conv_b_ref[:, cs]
        for tap in range(CONV_WIDTH - 1):
            back = CONV_WIDTH - 1 - tap
            conv = conv + buf[CONV_HALO - back:CONV_HALO - back + ts, :] * conv_w_ref[tap:tap + 1, cs]
        g_scr[:, cs] = (jax.nn.gelu(conv) * gate).astype(BF16)
    y = x + _dot(g_scr[...], w_down_ref[...])
    if final_norm:
        y = _rms(y) * g_out_ref[...]
    o_ref[...] = y


def _ffn(x, g_ffn, w_up, conv_w, conv_b, w_down, g_out, final_norm):
    b, s, d = x.shape
    ts = min(FFN_ROWS, s)
    assert s % ts == 0 and FFN_HIDDEN % FFN_COLS == 0
    row_tile = pl.BlockSpec((None, ts, d), lambda bi, si: (bi, si, 0))
    return pl.pallas_call(
        functools.partial(_ffn_kernel, final_norm=final_norm),
        grid=(b, s // ts),
        in_specs=[row_tile] + [_vmem_full()] * 6,
        out_specs=row_tile,
        out_shape=jax.ShapeDtypeStruct((b, s, d), F32),
        scratch_shapes=[
            pltpu.VMEM((ts, d), BF16),
            pltpu.VMEM((2, ts + CONV_HALO, FFN_COLS), F32),
            pltpu.VMEM((CONV_HALO, FFN_HIDDEN), F32),
            pltpu.VMEM((ts, FFN_HIDDEN), BF16),
        ],
        compiler_params=pltpu.CompilerParams(
            dimension_semantics=("arbitrary", "arbitrary"),
            vmem_limit_bytes=V7X_VMEM_LIMIT_BYTES),
        name="ffn",
    )(x, g_ffn, w_up, conv_w, conv_b, w_down, g_out)


def _block_diag(w):
    g, c, dd = w.shape
    out = jnp.zeros((g * c, g * dd), w.dtype)
    for i in range(g):
        out = out.at[i * c:(i + 1) * c, i * dd:(i + 1) * dd].set(w[i])
    return out


def kernel(x, mem, g_mix, w_in, w_pool, pool_scale, w_a, g_ret, b_ret, w_r, g_mem, w_mem_kv, w_c,
           w_out, g_ffn, w_up, conv_w, conv_b, w_down, g_final):
    depth = w_in.shape[0]
    tables = _retention_tables(x.shape[1])
    row = lambda a: a.reshape(1, -1)
    for l in range(depth):
        x = _mixer(x, mem, tables, row(g_mix[l]), w_in[l].astype(BF16),
                   _block_diag(w_pool[l]).astype(BF16), row(pool_scale[l]), w_a[l].astype(BF16),
                   row(g_ret[l]), row(b_ret[l]), w_r[l].astype(BF16), row(g_mem[l]),
                   w_mem_kv[l].astype(BF16), w_c[l].astype(BF16), w_out[l].astype(BF16))
        x = _ffn(x, row(g_ffn[l]), w_up[l].astype(BF16), conv_w[l], row(conv_b[l]),
                 w_down[l].astype(BF16), row(g_final), final_norm=(l == depth - 1))
    return x
```

```python
import functools

import jax
import jax.numpy as jnp
import numpy as np
from jax import lax
from jax.experimental import pallas as pl
from jax.experimental.pallas import tpu as pltpu

D_MODEL = 1024
POOL_WINDOWS = (2, 4, 8, 16)
POOL_GROUP_DIM = D_MODEL // 8
POOL_WIDTH = len(POOL_WINDOWS) * POOL_GROUP_DIM
POOL_HALO = 16
RET_HEADS = 4
RET_QK_DIM = D_MODEL // 8
RET_V_DIM = 2 * RET_QK_DIM
RET_QK_WIDTH = RET_HEADS * RET_QK_DIM
RET_V_WIDTH = RET_HEADS * RET_V_DIM
RET_CHUNK = 128
ROPE_BASE = 10000.0
XA_HEADS = 4
XA_HEAD_DIM = D_MODEL // 8
XA_WIDTH = XA_HEADS * XA_HEAD_DIM
FFN_HIDDEN = 2816
CONV_WIDTH = 3
CONV_HALO = 8
EPS = 1e-6

OFF_HP = 0
OFF_Q = OFF_HP + POOL_WIDTH
OFF_K = OFF_Q + RET_QK_WIDTH
OFF_V = OFF_K + RET_QK_WIDTH
OFF_GR = OFF_V + RET_V_WIDTH
OFF_QX = OFF_GR + RET_V_WIDTH
OFF_GATE = OFF_QX + XA_WIDTH

MIXER_ROWS = 512
FFN_ROWS = 1024
FFN_COLS = 256
V7X_VMEM_LIMIT_BYTES = 56 * 1024 * 1024

BF16 = jnp.bfloat16
F32 = jnp.float32


def _dot(a, b):
    return jnp.dot(a, b, preferred_element_type=F32)


def _rms(x):
    return x * lax.rsqrt(jnp.mean(x * x, axis=-1, keepdims=True) + EPS)


def _mixer_kernel(x_ref, mem_ref, cos_ref, sin_ref, dintra_ref, qdec_ref, kdec_ref, cdec_ref,
                  g_mix_ref, w_in_ref, w_pool_ref, pool_scale_ref, w_a_ref, g_ret_ref, b_ret_ref,
                  w_r_ref, g_mem_ref, w_mem_kv_ref, w_c_ref, w_out_ref,
                  o_ref,
                  h_scr, e_scr, r_scr, kt_scr, v_scr, ret_scr, m_scr):
    ts = x_ref.shape[0]
    s_idx = pl.program_id(1)

    @pl.when(s_idx == 0)
    def _start_of_sequence():
        e_scr[...] = jnp.zeros(e_scr.shape, F32)
        r_scr[...] = jnp.zeros(r_scr.shape, F32)
        mem_n = (_rms(mem_ref[...]) * g_mem_ref[...]).astype(BF16)
        kv = _dot(mem_n, w_mem_kv_ref[...])
        kt_scr[...] = kv[:, :XA_WIDTH].T.astype(BF16)
        v_scr[...] = kv[:, XA_WIDTH:].astype(BF16)

    x = x_ref[...]
    h_scr[...] = (_rms(x) * g_mix_ref[...]).astype(BF16)

    def proj(off, width):
        return _dot(h_scr[...], w_in_ref[:, off:off + width])

    def rope(a):
        cos = cos_ref[...]
        sin = sin_ref[...]
        parts = []
        for hh in range(RET_HEADS):
            ah = a[:, hh * RET_QK_DIM:(hh + 1) * RET_QK_DIM]
            parts.append(ah * cos + pltpu.roll(ah, RET_QK_DIM // 2, 1) * sin)
        return jnp.concatenate(parts, axis=-1)

    def qslice(hh):
        return slice(hh * RET_QK_DIM, (hh + 1) * RET_QK_DIM)

    def vslice(hh):
        return slice(hh * RET_V_DIM, (hh + 1) * RET_V_DIM)

    def retention_chunk(c, q, qd, k, kdt, v):
        rows = slice(c * RET_CHUNK, (c + 1) * RET_CHUNK)
        sc = [lax.dot_general(q[rows, qslice(hh)], k[rows, qslice(hh)], (((1,), (1,)), ((), ())),
                              preferred_element_type=F32) for hh in range(RET_HEADS)]
        state = [r_scr[:, vslice(hh)] for hh in range(RET_HEADS)]
        cross = [_dot(qd[rows, qslice(hh)], state[hh].astype(BF16)) for hh in range(RET_HEADS)]
        upd = [_dot(kdt[c][hh], v[rows, vslice(hh)]) for hh in range(RET_HEADS)]
        for hh in range(RET_HEADS):
            s_h = (sc[hh] * dintra_ref[hh]).astype(BF16)
            ret_scr[rows, vslice(hh)] = _dot(s_h, v[rows, vslice(hh)]) + cross[hh]
            r_scr[:, vslice(hh)] = cdec_ref[:, vslice(hh)] * state[hh] + upd[hh]

    n_chunks = ts // RET_CHUNK

    hp = proj(OFF_HP, POOL_WIDTH)
    q = proj(OFF_Q, RET_QK_WIDTH)
    k = proj(OFF_K, RET_QK_WIDTH)

    t1 = (lax.broadcasted_iota(jnp.int32, (ts, POOL_GROUP_DIM), 0) + (s_idx * ts + 1)).astype(F32)
    pooled = []
    for gi, w in enumerate(POOL_WINDOWS):
        cols = slice(gi * POOL_GROUP_DIM, (gi + 1) * POOL_GROUP_DIM)
        cur = hp[:, cols]
        acc = jnp.concatenate([e_scr[:, cols], cur], axis=0)
        span = 1
        while span < w:
            acc = acc + pltpu.roll(acc, span, 0)
            span *= 2
        pooled.append(acc[POOL_HALO:, :] / jnp.minimum(t1, float(w)) - cur)
    e_scr[...] = hp[ts - POOL_HALO:, :]
    pooled = jnp.concatenate(pooled, axis=-1).astype(BF16)

    v = proj(OFF_V, RET_V_WIDTH).astype(BF16)

    q = rope(q)
    k = rope(k) * (RET_QK_DIM ** -0.5)
    qd, kdt = [], []
    for c in range(n_chunks):
        rows = slice(c * RET_CHUNK, (c + 1) * RET_CHUNK)
        qd.append((q[rows, :] * qdec_ref[...]).astype(BF16))
        kd = k[rows, :] * kdec_ref[...]
        kdt.append([kd[:, qslice(hh)].T.astype(BF16) for hh in range(RET_HEADS)])
    qd = jnp.concatenate(qd, axis=0)
    q = q.astype(BF16)
    k = k.astype(BF16)

    y_pool = _dot(pooled, w_pool_ref[...]) * pool_scale_ref[...]
    qx = proj(OFF_QX, XA_WIDTH).astype(BF16)
    y_pool = _dot(y_pool.astype(BF16), w_a_ref[...])

    retention_chunk(0, q, qd, k, kdt, v)
    m_scr[...] = jax.nn.sigmoid(proj(OFF_GATE, D_MODEL)) * y_pool

    probs, denom = [], []
    for hh in range(XA_HEADS):
        hs = qslice(hh)
        sc = _dot(qx[:, hs], kt_scr[hs, :]) * (XA_HEAD_DIM ** -0.5)
        p = jnp.exp(sc - jnp.max(sc, axis=-1, keepdims=True))
        probs.append(p.astype(BF16))
        denom.append(jnp.sum(p, axis=-1, keepdims=True))

    gr = proj(OFF_GR, RET_V_WIDTH)
    for c in range(1, n_chunks):
        retention_chunk(c, q, qd, k, kdt, v)

    heads = [_dot(probs[hh], v_scr[:, qslice(hh)]) / denom[hh] for hh in range(XA_HEADS)]
    gate_mem = jax.nn.sigmoid(proj(OFF_GATE + 2 * D_MODEL, D_MODEL))
    y_mem = _dot(jnp.concatenate(heads, axis=-1).astype(BF16), w_c_ref[...])
    m_scr[...] += gate_mem * y_mem

    normed = []
    for hh in range(RET_HEADS):
        oh = ret_scr[:, vslice(hh)]
        dev = oh - jnp.mean(oh, axis=-1, keepdims=True)
        normed.append(dev * lax.rsqrt(jnp.mean(dev * dev, axis=-1, keepdims=True) + EPS))
    o = jnp.concatenate(normed, axis=-1) * g_ret_ref[...] + b_ret_ref[...]
    gate_ret = jax.nn.sigmoid(proj(OFF_GATE + D_MODEL, D_MODEL))
    y_ret = _dot((jax.nn.silu(gr) * o).astype(BF16), w_r_ref[...])
    merged = m_scr[...] + gate_ret * y_ret

    o_ref[...] = x + _dot(merged.astype(BF16), w_out_ref[...])


def _retention_tables(seq):
    half = RET_QK_DIM // 2
    pos = jnp.arange(seq, dtype=F32)
    inv = ROPE_BASE ** (-jnp.arange(half, dtype=F32) / half)
    ang = pos[:, None] * inv[None, :]
    cos, sin = jnp.cos(ang), jnp.sin(ang)
    cos2 = jnp.concatenate([cos, cos], axis=-1)
    sin2 = jnp.concatenate([-sin, sin], axis=-1)
    c = RET_CHUNK
    log_gamma = jnp.log1p(-jnp.exp2(-5.0 - jnp.arange(RET_HEADS, dtype=F32)))
    lg = log_gamma[:, None, None]
    idx = jnp.arange(c, dtype=F32)
    rel = idx[:, None] - idx[None, :]
    decay_intra = jnp.where(rel >= 0, jnp.exp(jnp.maximum(rel, 0.0) * lg), 0.0)
    q_decay = jnp.exp((idx + 1.0)[None, :, None] * lg)
    k_decay = jnp.exp((c - 1.0 - idx)[None, :, None] * lg)
    chunk_decay = jnp.exp(c * lg)

    def lanes(a, width):
        r = a.shape[1]
        return jnp.broadcast_to(a.transpose(1, 0, 2), (r, RET_HEADS, width)).reshape(r, RET_HEADS * width)

    return (cos2, sin2, decay_intra, lanes(q_decay, RET_QK_DIM), lanes(k_decay, RET_QK_DIM),
            lanes(chunk_decay, RET_V_DIM))


def _vmem_full():
    return pl.BlockSpec(memory_space=pltpu.VMEM)


def _mixer(x, mem, tables, g_mix, w_in, w_pool_bd, pool_scale, w_a, g_ret, b_ret, w_r, g_mem,
           w_mem_kv, w_c, w_out):
    b, s, d = x.shape
    m = mem.shape[1]
    ts = min(MIXER_ROWS, s)
    assert s % ts == 0 and ts % RET_CHUNK == 0
    cos2, sin2, dintra, qdec, kdec, cdec = tables
    row_tile = pl.BlockSpec((None, ts, d), lambda bi, si: (bi, si, 0))
    pos_tile = pl.BlockSpec((ts, RET_QK_DIM), lambda bi, si: (si, 0))
    in_specs = [row_tile,
                pl.BlockSpec((None, m, d), lambda bi, si: (bi, 0, 0)),
                pos_tile, pos_tile] + [_vmem_full()] * 16
    return pl.pallas_call(
        _mixer_kernel,
        grid=(b, s // ts),
        in_specs=in_specs,
        out_specs=row_tile,
        out_shape=jax.ShapeDtypeStruct((b, s, d), F32),
        scratch_shapes=[
            pltpu.VMEM((ts, d), BF16),
            pltpu.VMEM((POOL_HALO, POOL_WIDTH), F32),
            pltpu.VMEM((RET_QK_DIM, RET_V_WIDTH), F32),
            pltpu.VMEM((XA_WIDTH, m), BF16),
            pltpu.VMEM((m, XA_WIDTH), BF16),
            pltpu.VMEM((ts, RET_V_WIDTH), F32),
            pltpu.VMEM((ts, d), F32),
        ],
        compiler_params=pltpu.CompilerParams(
            dimension_semantics=("arbitrary", "arbitrary"),
            vmem_limit_bytes=V7X_VMEM_LIMIT_BYTES),
        name="mixer",
    )(x, mem, cos2, sin2, dintra, qdec, kdec, cdec,
      g_mix, w_in, w_pool_bd, pool_scale, w_a, g_ret, b_ret, w_r, g_mem, w_mem_kv, w_c, w_out)


def _ffn_kernel(x_ref, g_ffn_ref, w_up_ref, conv_w_ref, conv_b_ref, w_down_ref, g_out_ref,
                o_ref, h_scr, a_scr, carry_scr, g_scr, *, final_norm):
    ts = x_ref.shape[0]

    @pl.when(pl.program_id(1) == 0)
    def _start_of_sequence():
        carry_scr[...] = jnp.zeros(carry_scr.shape, F32)

    x = x_ref[...]
    h_scr[...] = (_rms(x) * g_ffn_ref[...]).astype(BF16)
    for j in range(FFN_HIDDEN // FFN_COLS):
        cs = slice(j * FFN_COLS, (j + 1) * FFN_COLS)
        a = _dot(h_scr[...], w_up_ref[:, cs])
        gate = _dot(h_scr[...], w_up_ref[:, FFN_HIDDEN + j * FFN_COLS:FFN_HIDDEN + (j + 1) * FFN_COLS])
        buf = a_scr.at[j % 2]
        buf[0:CONV_HALO, :] = carry_scr[:, cs]
        buf[CONV_HALO:CONV_HALO + ts, :] = a
        carry_scr[:, cs] = a[ts - CONV_HALO:, :]
        conv = a * conv_w_ref[CONV_WIDTH - 1:CONV_WIDTH, cs] + conv_b_ref[:, cs]
        for tap in range(CONV_WIDTH - 1):
            back = CONV_WIDTH - 1 - tap
            conv = conv + buf[CONV_HALO - back:CONV_HALO - back + ts, :] * conv_w_ref[tap:tap + 1, cs]
        g_scr[:, cs] = (jax.nn.gelu(conv) * gate).astype(BF16)
    y = x + _dot(g_scr[...], w_down_ref[...])
    if final_norm:
        y = _rms(y) * g_out_ref[...]
    o_ref[...] = y


def _ffn(x, g_ffn, w_up, conv_w, conv_b, w_down, g_out, final_norm):
    b, s, d = x.shape
    ts = min(FFN_ROWS, s)
    assert s % ts == 0 and FFN_HIDDEN % FFN_COLS == 0
    row_tile = pl.BlockSpec((None, ts, d), lambda bi, si: (bi, si, 0))
    return pl.pallas_call(
        functools.partial(_ffn_kernel, final_norm=final_norm),
        grid=(b, s // ts),
        in_specs=[row_tile] + [_vmem_full()] * 6,
        out_specs=row_tile,
        out_shape=jax.ShapeDtypeStruct((b, s, d), F32),
        scratch_shapes=[
            pltpu.VMEM((ts, d), BF16),
            pltpu.VMEM((2, ts + CONV_HALO, FFN_COLS), F32),
            pltpu.VMEM((CONV_HALO, FFN_HIDDEN), F32),
            pltpu.VMEM((ts, FFN_HIDDEN), BF16),
        ],
        compiler_params=pltpu.CompilerParams(
            dimension_semantics=("arbitrary", "arbitrary"),
            vmem_limit_bytes=V7X_VMEM_LIMIT_BYTES),
        name="ffn",
    )(x, g_ffn, w_up, conv_w, conv_b, w_down, g_out)


def _block_diag(w):
    g, c, dd = w.shape
    out = jnp.zeros((g * c, g * dd), w.dtype)
    for i in range(g):
        out = out.at[i * c:(i + 1) * c, i * dd:(i + 1) * dd].set(w[i])
    return out


def kernel(x, mem, g_mix, w_in, w_pool, pool_scale, w_a, g_ret, b_ret, w_r, g_mem, w_mem_kv, w_c,
           w_out, g_ffn, w_up, conv_w, conv_b, w_down, g_final):
    depth = w_in.shape[0]
    tables = _retention_tables(x.shape[1])
    row = lambda a: a.reshape(1, -1)
    for l in range(depth):
        x = _mixer(x, mem, tables, row(g_mix[l]), w_in[l].astype(BF16),
                   _block_diag(w_pool[l]).astype(BF16), row(pool_scale[l]), w_a[l].astype(BF16),
                   row(g_ret[l]), row(b_ret[l]), w_r[l].astype(BF16), row(g_mem[l]),
                   w_mem_kv[l].astype(BF16), w_c[l].astype(BF16), w_out[l].astype(BF16))
        x = _ffn(x, row(g_ffn[l]), w_up[l].astype(BF16), conv_w[l], row(conv_b[l]),
                 w_down[l].astype(BF16), row(g_final), final_norm=(l == depth - 1))
    return x
```

```python
import functools

import jax
import jax.numpy as jnp
import numpy as np
from jax import lax
from jax.experimental import pallas as pl
from jax.experimental.pallas import tpu as pltpu

D_MODEL = 1024
POOL_WINDOWS = (2, 4, 8, 16)
POOL_GROUP_DIM = D_MODEL // 8
POOL_WIDTH = len(POOL_WINDOWS) * POOL_GROUP_DIM
POOL_HALO = 16
RET_HEADS = 4
RET_QK_DIM = D_MODEL // 8
RET_V_DIM = 2 * RET_QK_DIM
RET_QK_WIDTH = RET_HEADS * RET_QK_DIM
RET_V_WIDTH = RET_HEADS * RET_V_DIM
RET_CHUNK = 128
ROPE_BASE = 10000.0
XA_HEADS = 4
XA_HEAD_DIM = D_MODEL // 8
XA_WIDTH = XA_HEADS * XA_HEAD_DIM
FFN_HIDDEN = 2816
CONV_WIDTH = 3
CONV_HALO = 8
EPS = 1e-6

OFF_HP = 0
OFF_Q = OFF_HP + POOL_WIDTH
OFF_K = OFF_Q + RET_QK_WIDTH
OFF_V = OFF_K + RET_QK_WIDTH
OFF_GR = OFF_V + RET_V_WIDTH
OFF_QX = OFF_GR + RET_V_WIDTH
OFF_GATE = OFF_QX + XA_WIDTH

MIXER_ROWS = 512
FFN_ROWS = 1024
FFN_COLS = 256
OUT_ROWS = 256
V7X_VMEM_LIMIT_BYTES = 56 * 1024 * 1024

BF16 = jnp.bfloat16
F32 = jnp.float32


def _dot(a, b):
    return jnp.dot(a, b, preferred_element_type=F32)


def _rms(x):
    return x * lax.rsqrt(jnp.mean(x * x, axis=-1, keepdims=True) + EPS)


def _sigmoid(x):
    return 0.5 * jnp.tanh(0.5 * x) + 0.5


def _mixer_kernel(x_ref, mem_ref, cos_ref, sin_ref, cost_ref, sint_ref, dintra_ref, qdec_ref, kdect_ref,
                  cdec_ref, g_mix_ref, w_in_ref, w_kt_ref, w_pool_ref, pool_scale_ref, w_a_ref, g_ret_ref, b_ret_ref,
                  w_r_ref, g_mem_ref, w_mem_kv_ref, w_c_ref, w_out_ref,
                  o_ref,
                  h_scr, e_scr, r_scr, kt_scr, v_scr, ret_scr, m_scr):
    ts = x_ref.shape[0]
    s_idx = pl.program_id(1)

    @pl.when(s_idx == 0)
    def _start_of_sequence():
        e_scr[...] = jnp.zeros(e_scr.shape, F32)
        r_scr[...] = jnp.zeros(r_scr.shape, F32)
        mem_n = (_rms(mem_ref[...]) * g_mem_ref[...]).astype(BF16)
        kv = _dot(mem_n, w_mem_kv_ref[...])
        kt_scr[...] = kv[:, :XA_WIDTH].T.astype(BF16)
        v_scr[...] = kv[:, XA_WIDTH:].astype(BF16)

    x = x_ref[...]
    h_scr[...] = (_rms(x) * g_mix_ref[...]).astype(BF16)

    def proj(off, width):
        return _dot(h_scr[...], w_in_ref[:, off:off + width])

    def rope(a):
        cos = cos_ref[...]
        sin = sin_ref[...]
        parts = []
        for hh in range(RET_HEADS):
            ah = a[:, hh * RET_QK_DIM:(hh + 1) * RET_QK_DIM]
            parts.append(ah * cos + pltpu.roll(ah, RET_QK_DIM // 2, 1) * sin)
        return jnp.concatenate(parts, axis=-1)

    def qslice(hh):
        return slice(hh * RET_QK_DIM, (hh + 1) * RET_QK_DIM)

    def vslice(hh):
        return slice(hh * RET_V_DIM, (hh + 1) * RET_V_DIM)

    def retention_chunk(c, qd, kdt, v):
        rows = slice(c * RET_CHUNK, (c + 1) * RET_CHUNK)
        zero = jnp.zeros((RET_QK_DIM, RET_QK_DIM), BF16)
        sc = []
        for h0 in range(0, RET_HEADS, 2):
            keys = jnp.concatenate([jnp.concatenate([kdt[c][h0], zero], axis=1),
                                    jnp.concatenate([zero, kdt[c][h0 + 1]], axis=1)], axis=0)
            pair = _dot(qd[rows, h0 * RET_QK_DIM:(h0 + 2) * RET_QK_DIM], keys)
            sc += [pair[:, :RET_QK_DIM], pair[:, RET_QK_DIM:]]
        state = [r_scr[:, vslice(hh)] for hh in range(RET_HEADS)]
        upd = [_dot(kdt[c][hh], v[rows, vslice(hh)]) for hh in range(RET_HEADS)]
        for hh in range(RET_HEADS):
            s_h = (sc[hh] * dintra_ref[hh]).astype(BF16)
            lhs = jnp.concatenate([s_h, qd[rows, qslice(hh)]], axis=1)
            rhs = jnp.concatenate([v[rows, vslice(hh)], state[hh].astype(BF16)], axis=0)
            ret_scr[rows, vslice(hh)] = _dot(lhs, rhs)
            r_scr[:, vslice(hh)] = cdec_ref[:, vslice(hh)] * state[hh] + upd[hh]

    n_chunks = ts // RET_CHUNK

    hp = proj(OFF_HP, POOL_WIDTH)
    q = proj(OFF_Q, RET_QK_WIDTH)
    kt = lax.dot_general(w_kt_ref[...], h_scr[...], (((1,), (1,)), ((), ())), preferred_element_type=F32)

    t1 = (lax.broadcasted_iota(jnp.int32, (ts, POOL_GROUP_DIM), 0) + (s_idx * ts + 1)).astype(F32)
    pooled = []
    for gi, w in enumerate(POOL_WINDOWS):
        cols = slice(gi * POOL_GROUP_DIM, (gi + 1) * POOL_GROUP_DIM)
        cur = hp[:, cols]
        acc = jnp.concatenate([e_scr[:, cols], cur], axis=0)
        span = 1
        while span < w:
            acc = acc + pltpu.roll(acc, span, 0)
            span *= 2
        pooled.append(acc[POOL_HALO:, :] / jnp.minimum(t1, float(w)) - cur)
    e_scr[...] = hp[ts - POOL_HALO:, :]
    pooled = jnp.concatenate(pooled, axis=-1).astype(BF16)

    v = proj(OFF_V, RET_V_WIDTH).astype(BF16)

    q = rope(q)
    qd = jnp.concatenate([(q[c * RET_CHUNK:(c + 1) * RET_CHUNK, :] * qdec_ref[...]).astype(BF16)
                          for c in range(n_chunks)], axis=0)
    half = RET_QK_DIM // 2
    kdt = [[] for _ in range(n_chunks)]
    for hh in range(RET_HEADS):
        kh = kt[qslice(hh), :]
        swapped = jnp.concatenate([kh[half:, :], kh[:half, :]], axis=0)
        kh = (kh * cost_ref[...] + swapped * sint_ref[...]) * (RET_QK_DIM ** -0.5)
        for c in range(n_chunks):
            kdt[c].append((kh[:, c * RET_CHUNK:(c + 1) * RET_CHUNK] * kdect_ref[qslice(hh), :]).astype(BF16))

    pair_w = 2 * POOL_GROUP_DIM
    y_pool = jnp.concatenate(
        [_dot(pooled[:, i * pair_w:(i + 1) * pair_w], w_pool_ref[i]) for i in range(POOL_WIDTH // pair_w)],
        axis=-1) * pool_scale_ref[...]
    qx = proj(OFF_QX, XA_WIDTH).astype(BF16)
    y_pool = _dot(y_pool.astype(BF16), w_a_ref[...])

    retention_chunk(0, qd, kdt, v)
    m_scr[...] = _sigmoid(proj(OFF_GATE, D_MODEL)) * y_pool

    probs, denom = [], []
    for hh in range(XA_HEADS):
        hs = qslice(hh)
        sc = _dot(qx[:, hs], kt_scr[hs, :]) * (XA_HEAD_DIM ** -0.5)
        p = jnp.exp(sc - jnp.max(sc, axis=-1, keepdims=True))
        probs.append(p.astype(BF16))
        denom.append(jnp.sum(p, axis=-1, keepdims=True))

    gr = proj(OFF_GR, RET_V_WIDTH)
    for c in range(1, n_chunks):
        retention_chunk(c, qd, kdt, v)

    heads = [_dot(probs[hh], v_scr[:, qslice(hh)]) / denom[hh] for hh in range(XA_HEADS)]
    gate_mem = _sigmoid(proj(OFF_GATE + 2 * D_MODEL, D_MODEL))
    y_mem = _dot(jnp.concatenate(heads, axis=-1).astype(BF16), w_c_ref[...])
    m_scr[...] += gate_mem * y_mem

    normed = []
    for hh in range(RET_HEADS):
        oh = ret_scr[:, vslice(hh)]
        dev = oh - jnp.mean(oh, axis=-1, keepdims=True)
        normed.append(dev * lax.rsqrt(jnp.mean(dev * dev, axis=-1, keepdims=True) + EPS))
    o = jnp.concatenate(normed, axis=-1) * g_ret_ref[...] + b_ret_ref[...]
    gate_ret = _sigmoid(proj(OFF_GATE + D_MODEL, D_MODEL))
    y_ret = _dot((gr * _sigmoid(gr) * o).astype(BF16), w_r_ref[...])
    merged = (m_scr[...] + gate_ret * y_ret).astype(BF16)

    for r0 in range(0, ts, OUT_ROWS):
        rows = slice(r0, r0 + OUT_ROWS)
        o_ref[rows, :] = x_ref[rows, :] + _dot(merged[rows, :], w_out_ref[...])


def _retention_tables(seq):
    half = RET_QK_DIM // 2
    pos = jnp.arange(seq, dtype=F32)
    inv = ROPE_BASE ** (-jnp.arange(half, dtype=F32) / half)
    ang = pos[:, None] * inv[None, :]
    cos, sin = jnp.cos(ang), jnp.sin(ang)
    cos2 = jnp.concatenate([cos, cos], axis=-1)
    sin2 = jnp.concatenate([-sin, sin], axis=-1)
    cos2t, sin2t = cos2.T, sin2.T
    c = RET_CHUNK
    log_gamma = jnp.log1p(-jnp.exp2(-5.0 - jnp.arange(RET_HEADS, dtype=F32)))
    lg = log_gamma[:, None, None]
    idx = jnp.arange(c, dtype=F32)
    rel = idx[:, None] - idx[None, :]
    q_decay = jnp.exp((idx + 1.0)[None, :, None] * lg)
    k_decay = jnp.exp((c - 1.0 - idx)[None, :, None] * lg)
    chunk_decay = jnp.exp(c * lg)
    decay_intra = jnp.where(rel >= 0, 1.0 / chunk_decay, 0.0)

    def lanes(a, width):
        r = a.shape[1]
        return jnp.broadcast_to(a.transpose(1, 0, 2), (r, RET_HEADS, width)).reshape(r, RET_HEADS * width)

    k_decay_t = jnp.broadcast_to(k_decay.transpose(0, 2, 1), (RET_HEADS, RET_QK_DIM, c))
    return (cos2, sin2, cos2t, sin2t, decay_intra, lanes(q_decay, RET_QK_DIM),
            k_decay_t.reshape(RET_QK_WIDTH, c), lanes(chunk_decay, RET_V_DIM))


def _vmem_full():
    return pl.BlockSpec(memory_space=pltpu.VMEM)


def _mixer(x, mem, tables, g_mix, w_in, w_kt, w_pool_bd, pool_scale, w_a, g_ret, b_ret, w_r, g_mem,
           w_mem_kv, w_c, w_out):
    b, s, d = x.shape
    m = mem.shape[1]
    ts = min(MIXER_ROWS, s)
    assert s % ts == 0 and ts % RET_CHUNK == 0 and ts % OUT_ROWS == 0
    cos2, sin2, cos2t, sin2t, dintra, qdec, kdect, cdec = tables
    row_tile = pl.BlockSpec((None, ts, d), lambda bi, si: (bi, si, 0))
    pos_tile = pl.BlockSpec((ts, RET_QK_DIM), lambda bi, si: (si, 0))
    pos_tile_t = pl.BlockSpec((RET_QK_DIM, ts), lambda bi, si: (0, si))
    in_specs = [row_tile,
                pl.BlockSpec((None, m, d), lambda bi, si: (bi, 0, 0)),
                pos_tile, pos_tile, pos_tile_t, pos_tile_t] + [_vmem_full()] * 17
    return pl.pallas_call(
        _mixer_kernel,
        grid=(b, s // ts),
        in_specs=in_specs,
        out_specs=row_tile,
        out_shape=jax.ShapeDtypeStruct((b, s, d), F32),
        scratch_shapes=[
            pltpu.VMEM((ts, d), BF16),
            pltpu.VMEM((POOL_HALO, POOL_WIDTH), F32),
            pltpu.VMEM((RET_QK_DIM, RET_V_WIDTH), F32),
            pltpu.VMEM((XA_WIDTH, m), BF16),
            pltpu.VMEM((m, XA_WIDTH), BF16),
            pltpu.VMEM((ts, RET_V_WIDTH), F32),
            pltpu.VMEM((ts, d), F32),
        ],
        compiler_params=pltpu.CompilerParams(
            dimension_semantics=("arbitrary", "arbitrary"),
            vmem_limit_bytes=V7X_VMEM_LIMIT_BYTES),
        name="mixer",
    )(x, mem, cos2, sin2, cos2t, sin2t, dintra, qdec, kdect, cdec,
      g_mix, w_in, w_kt, w_pool_bd, pool_scale, w_a, g_ret, b_ret, w_r, g_mem, w_mem_kv, w_c, w_out)


def _ffn_kernel(x_ref, g_ffn_ref, w_up_ref, conv_w_ref, conv_b_ref, w_down_ref, g_out_ref,
                o_ref, h_scr, a_scr, carry_scr, g_scr, *, final_norm):
    ts = x_ref.shape[0]

    @pl.when(pl.program_id(1) == 0)
    def _start_of_sequence():
        carry_scr[...] = jnp.zeros(carry_scr.shape, F32)

    x = x_ref[...]
    h_scr[...] = (_rms(x) * g_ffn_ref[...]).astype(BF16)
    for j in range(FFN_HIDDEN // FFN_COLS):
        cs = slice(j * FFN_COLS, (j + 1) * FFN_COLS)
        a = _dot(h_scr[...], w_up_ref[:, cs])
        gate = _dot(h_scr[...], w_up_ref[:, FFN_HIDDEN + j * FFN_COLS:FFN_HIDDEN + (j + 1) * FFN_COLS])
        buf = a_scr.at[j % 2]
        buf[0:CONV_HALO, :] = carry_scr[:, cs]
        buf[CONV_HALO:CONV_HALO + ts, :] = a
        carry_scr[:, cs] = a[ts - CONV_HALO:, :]
        conv = a * conv_w_ref[CONV_WIDTH - 1:CONV_WIDTH, cs] + conv_b_ref[:, cs]
        for tap in range(CONV_WIDTH - 1):
            back = CONV_WIDTH - 1 - tap
            conv = conv + buf[CONV_HALO - back:CONV_HALO - back + ts, :] * conv_w_ref[tap:tap + 1, cs]
        g_scr[:, cs] = (jax.nn.gelu(conv) * gate).astype(BF16)
    for r0 in range(0, ts, OUT_ROWS):
        rows = slice(r0, r0 + OUT_ROWS)
        y = x_ref[rows, :] + _dot(g_scr[rows, :], w_down_ref[...])
        if final_norm:
            y = _rms(y) * g_out_ref[...]
        o_ref[rows, :] = y


def _ffn(x, g_ffn, w_up, conv_w, conv_b, w_down, g_out, final_norm):
    b, s, d = x.shape
    ts = min(FFN_ROWS, s)
    assert s % ts == 0 and FFN_HIDDEN % FFN_COLS == 0 and ts % OUT_ROWS == 0
    row_tile = pl.BlockSpec((None, ts, d), lambda bi, si: (bi, si, 0))
    return pl.pallas_call(
        functools.partial(_ffn_kernel, final_norm=final_norm),
        grid=(b, s // ts),
        in_specs=[row_tile] + [_vmem_full()] * 6,
        out_specs=row_tile,
        out_shape=jax.ShapeDtypeStruct((b, s, d), F32),
        scratch_shapes=[
            pltpu.VMEM((ts, d), BF16),
            pltpu.VMEM((2, ts + CONV_HALO, FFN_COLS), F32),
            pltpu.VMEM((CONV_HALO, FFN_HIDDEN), F32),
            pltpu.VMEM((ts, FFN_HIDDEN), BF16),
        ],
        compiler_params=pltpu.CompilerParams(
            dimension_semantics=("arbitrary", "arbitrary"),
            vmem_limit_bytes=V7X_VMEM_LIMIT_BYTES),
        name="ffn",
    )(x, g_ffn, w_up, conv_w, conv_b, w_down, g_out)


def _pair_block_diag(w):
    g, c, dd = w.shape
    zero = jnp.zeros((g // 2, c, dd), w.dtype)
    top = jnp.concatenate([w[0::2], zero], axis=2)
    bottom = jnp.concatenate([zero, w[1::2]], axis=2)
    return jnp.concatenate([top, bottom], axis=1)


def kernel(x, mem, g_mix, w_in, w_pool, pool_scale, w_a, g_ret, b_ret, w_r, g_mem, w_mem_kv, w_c,
           w_out, g_ffn, w_up, conv_w, conv_b, w_down, g_final):
    depth = w_in.shape[0]
    tables = _retention_tables(x.shape[1])
    row = lambda a: a.reshape(1, -1)
    for l in range(depth):
        w_kt = w_in[l][:, OFF_K:OFF_K + RET_QK_WIDTH].T.astype(BF16)
        x = _mixer(x, mem, tables, row(g_mix[l]), w_in[l].astype(BF16), w_kt,
                   _pair_block_diag(w_pool[l]).astype(BF16), row(pool_scale[l]), w_a[l].astype(BF16),
                   row(g_ret[l]), row(b_ret[l]), w_r[l].astype(BF16), row(g_mem[l]),
                   w_mem_kv[l].astype(BF16), w_c[l].astype(BF16), w_out[l].astype(BF16))
        x = _ffn(x, row(g_ffn[l]), w_up[l].astype(BF16), conv_w[l], row(conv_b[l]),
                 w_down[l].astype(BF16), row(g_final), final_norm=(l == depth - 1))
    return x
```

```python
import functools

import jax
import jax.numpy as jnp
from jax import lax
from jax.experimental import pallas as pl
from jax.experimental.pallas import tpu as pltpu

D_MODEL = 1024
POOL_WINDOWS = (2, 4, 8, 16)
POOL_GROUP_DIM = D_MODEL // 8
POOL_WIDTH = len(POOL_WINDOWS) * POOL_GROUP_DIM
POOL_HALO = 16
RET_HEADS = 4
RET_QK_DIM = D_MODEL // 8
RET_V_DIM = 2 * RET_QK_DIM
RET_QK_WIDTH = RET_HEADS * RET_QK_DIM
RET_V_WIDTH = RET_HEADS * RET_V_DIM
RET_CHUNK = 128
ROPE_BASE = 10000.0
XA_HEADS = 4
XA_HEAD_DIM = D_MODEL // 8
XA_WIDTH = XA_HEADS * XA_HEAD_DIM
FFN_HIDDEN = 2816
CONV_WIDTH = 3
CONV_HALO = 8
BF16_ROWS = 16
EPS = 1e-6

OFF_HP = 0
OFF_Q = OFF_HP + POOL_WIDTH
OFF_K = OFF_Q + RET_QK_WIDTH
OFF_V = OFF_K + RET_QK_WIDTH
OFF_GR = OFF_V + RET_V_WIDTH
OFF_QX = OFF_GR + RET_V_WIDTH
OFF_GATE = OFF_QX + XA_WIDTH

MIXER_ROWS = 512
FFN_ROWS = 1024
FFN_COLS = 256
OUT_ROWS = 256
V7X_VMEM_LIMIT_BYTES = 56 * 1024 * 1024

BF16 = jnp.bfloat16
F32 = jnp.float32


def _dot(a, b):
    return jnp.dot(a, b, preferred_element_type=F32)


def _rms(x):
    return x * lax.rsqrt(jnp.mean(x * x, axis=-1, keepdims=True) + EPS)


def _sigmoid(x):
    return 0.5 * jnp.tanh(0.5 * x) + 0.5


def _mixer_kernel(x_ref, mem_ref, cos_ref, sin_ref, cost_ref, sint_ref, dintra_ref, qdec_ref, kdect_ref,
                  cdec_ref, g_mix_ref, w_in_ref, w_kt_ref, w_pool_ref, pool_scale_ref, w_a_ref, g_ret_ref, b_ret_ref,
                  w_r_ref, g_mem_ref, w_mem_kv_ref, w_c_ref, w_out_ref, w_up_ref, w_down_ref,
                  o_ref, w_up_bf_ref, w_down_bf_ref,
                  h_scr, e_scr, r_scr, kt_scr, v_scr, ret_scr, m_scr, *, up_chunks, down_chunks):
    ts = x_ref.shape[0]
    s_idx = pl.program_id(1)
    step = pl.program_id(0) * pl.num_programs(1) + s_idx

    @pl.when(step < up_chunks)
    def _cast_up():
        w_up_bf_ref[...] = w_up_ref[...].astype(BF16)

    @pl.when(step < down_chunks)
    def _cast_down():
        w_down_bf_ref[...] = w_down_ref[...].astype(BF16)

    @pl.when(s_idx == 0)
    def _start_of_sequence():
        e_scr[...] = jnp.zeros(e_scr.shape, F32)
        r_scr[...] = jnp.zeros(r_scr.shape, F32)
        mem_n = (_rms(mem_ref[...]) * g_mem_ref[...]).astype(BF16)
        kv = _dot(mem_n, w_mem_kv_ref[...])
        kt_scr[...] = kv[:, :XA_WIDTH].T.astype(BF16)
        v_scr[...] = kv[:, XA_WIDTH:].astype(BF16)

    h_scr[...] = (_rms(x_ref[...]) * g_mix_ref[...]).astype(BF16)

    def proj(off, width):
        return _dot(h_scr[...], w_in_ref[:, off:off + width])

    def rope(a):
        cos = cos_ref[...]
        sin = sin_ref[...]
        parts = []
        for hh in range(RET_HEADS):
            ah = a[:, hh * RET_QK_DIM:(hh + 1) * RET_QK_DIM]
            parts.append(ah * cos + pltpu.roll(ah, RET_QK_DIM // 2, 1) * sin)
        return jnp.concatenate(parts, axis=-1)

    def qslice(hh):
        return slice(hh * RET_QK_DIM, (hh + 1) * RET_QK_DIM)

    def vslice(hh):
        return slice(hh * RET_V_DIM, (hh + 1) * RET_V_DIM)

    def retention_chunk(c, qd, kdt, v):
        rows = slice(c * RET_CHUNK, (c + 1) * RET_CHUNK)
        zero = jnp.zeros((RET_QK_DIM, RET_QK_DIM), BF16)
        sc = []
        for h0 in range(0, RET_HEADS, 2):
            keys = jnp.concatenate([jnp.concatenate([kdt[c][h0], zero], axis=1),
                                    jnp.concatenate([zero, kdt[c][h0 + 1]], axis=1)], axis=0)
            pair = _dot(qd[rows, h0 * RET_QK_DIM:(h0 + 2) * RET_QK_DIM], keys)
            sc += [pair[:, :RET_QK_DIM], pair[:, RET_QK_DIM:]]
        state = [r_scr[:, vslice(hh)] for hh in range(RET_HEADS)]
        upd = [_dot(kdt[c][hh], v[rows, vslice(hh)]) for hh in range(RET_HEADS)]
        for hh in range(RET_HEADS):
            s_h = (sc[hh] * dintra_ref[hh]).astype(BF16)
            lhs = jnp.concatenate([s_h, qd[rows, qslice(hh)]], axis=1)
            rhs = jnp.concatenate([v[rows, vslice(hh)], state[hh].astype(BF16)], axis=0)
            ret_scr[rows, vslice(hh)] = _dot(lhs, rhs)
            r_scr[:, vslice(hh)] = cdec_ref[:, vslice(hh)] * state[hh] + upd[hh]

    n_chunks = ts // RET_CHUNK

    hp = proj(OFF_HP, POOL_WIDTH)
    q = proj(OFF_Q, RET_QK_WIDTH)
    kt = lax.dot_general(w_kt_ref[...], h_scr[...], (((1,), (1,)), ((), ())), preferred_element_type=F32)

    t1 = (lax.broadcasted_iota(jnp.int32, (ts, POOL_GROUP_DIM), 0) + (s_idx * ts + 1)).astype(F32)
    pooled = []
    for gi, w in enumerate(POOL_WINDOWS):
        cols = slice(gi * POOL_GROUP_DIM, (gi + 1) * POOL_GROUP_DIM)
        cur = hp[:, cols]
        acc = jnp.concatenate([e_scr[:, cols], cur], axis=0)
        span = 1
        while span < w:
            acc = acc + pltpu.roll(acc, span, 0)
            span *= 2
        pooled.append(acc[POOL_HALO:, :] / jnp.minimum(t1, float(w)) - cur)
    e_scr[...] = hp[ts - POOL_HALO:, :]
    pooled = jnp.concatenate(pooled, axis=-1).astype(BF16)

    v = proj(OFF_V, RET_V_WIDTH).astype(BF16)

    q = rope(q)
    qd = jnp.concatenate([(q[c * RET_CHUNK:(c + 1) * RET_CHUNK, :] * qdec_ref[...]).astype(BF16)
                          for c in range(n_chunks)], axis=0)
    half = RET_QK_DIM // 2
    kdt = [[] for _ in range(n_chunks)]
    for hh in range(RET_HEADS):
        kh = kt[qslice(hh), :]
        swapped = jnp.concatenate([kh[half:, :], kh[:half, :]], axis=0)
        kh = (kh * cost_ref[...] + swapped * sint_ref[...]) * (RET_QK_DIM ** -0.5)
        for c in range(n_chunks):
            kdt[c].append((kh[:, c * RET_CHUNK:(c + 1) * RET_CHUNK] * kdect_ref[qslice(hh), :]).astype(BF16))

    pair_w = 2 * POOL_GROUP_DIM
    y_pool = jnp.concatenate(
        [_dot(pooled[:, i * pair_w:(i + 1) * pair_w], w_pool_ref[i]) for i in range(POOL_WIDTH // pair_w)],
        axis=-1) * pool_scale_ref[...]
    qx = proj(OFF_QX, XA_WIDTH).astype(BF16)
    y_pool = _dot(y_pool.astype(BF16), w_a_ref[...])

    retention_chunk(0, qd, kdt, v)
    m_scr[...] = _sigmoid(proj(OFF_GATE, D_MODEL)) * y_pool

    probs, denom = [], []
    for hh in range(XA_HEADS):
        hs = qslice(hh)
        sc = _dot(qx[:, hs], kt_scr[hs, :]) * (XA_HEAD_DIM ** -0.5)
        p = jnp.exp(sc - jnp.max(sc, axis=-1, keepdims=True))
        probs.append(p.astype(BF16))
        denom.append(jnp.sum(p, axis=-1, keepdims=True))

    gr = proj(OFF_GR, RET_V_WIDTH)
    for c in range(1, n_chunks):
        retention_chunk(c, qd, kdt, v)

    heads = [_dot(probs[hh], v_scr[:, qslice(hh)]) / denom[hh] for hh in range(XA_HEADS)]
    gate_mem = _sigmoid(proj(OFF_GATE + 2 * D_MODEL, D_MODEL))
    y_mem = _dot(jnp.concatenate(heads, axis=-1).astype(BF16), w_c_ref[...])
    m_scr[...] += gate_mem * y_mem

    normed = []
    for hh in range(RET_HEADS):
        oh = ret_scr[:, vslice(hh)]
        dev = oh - jnp.mean(oh, axis=-1, keepdims=True)
        normed.append(dev * lax.rsqrt(jnp.mean(dev * dev, axis=-1, keepdims=True) + EPS))
    o = jnp.concatenate(normed, axis=-1) * g_ret_ref[...] + b_ret_ref[...]
    gate_ret = _sigmoid(proj(OFF_GATE + D_MODEL, D_MODEL))
    y_ret = _dot((gr * _sigmoid(gr) * o).astype(BF16), w_r_ref[...])
    merged = (m_scr[...] + gate_ret * y_ret).astype(BF16)

    for r0 in range(0, ts, OUT_ROWS):
        rows = slice(r0, r0 + OUT_ROWS)
        o_ref[rows, :] = x_ref[rows, :] + _dot(merged[rows, :], w_out_ref[...])


def _retention_tables(seq):
    half = RET_QK_DIM // 2
    pos = jnp.arange(seq, dtype=F32)
    inv = ROPE_BASE ** (-jnp.arange(half, dtype=F32) / half)
    ang = pos[:, None] * inv[None, :]
    cos, sin = jnp.cos(ang), jnp.sin(ang)
    cos2 = jnp.concatenate([cos, cos], axis=-1)
    sin2 = jnp.concatenate([-sin, sin], axis=-1)
    cos2t, sin2t = cos2.T, sin2.T
    c = RET_CHUNK
    log_gamma = jnp.log1p(-jnp.exp2(-5.0 - jnp.arange(RET_HEADS, dtype=F32)))
    lg = log_gamma[:, None, None]
    idx = jnp.arange(c, dtype=F32)
    rel = idx[:, None] - idx[None, :]
    q_decay = jnp.exp((idx + 1.0)[None, :, None] * lg)
    k_decay = jnp.exp((c - 1.0 - idx)[None, :, None] * lg)
    chunk_decay = jnp.exp(c * lg)
    decay_intra = jnp.where(rel >= 0, 1.0 / chunk_decay, 0.0)

    def lanes(a, width):
        r = a.shape[1]
        return jnp.broadcast_to(a.transpose(1, 0, 2), (r, RET_HEADS, width)).reshape(r, RET_HEADS * width)

    k_decay_t = jnp.broadcast_to(k_decay.transpose(0, 2, 1), (RET_HEADS, RET_QK_DIM, c))
    return (cos2, sin2, cos2t, sin2t, decay_intra, lanes(q_decay, RET_QK_DIM),
            k_decay_t.reshape(RET_QK_WIDTH, c), lanes(chunk_decay, RET_V_DIM))


def _vmem_full():
    return pl.BlockSpec(memory_space=pltpu.VMEM)


def _cast_chunks(rows, n_steps):
    tiles = rows // BF16_ROWS
    assert rows % BF16_ROWS == 0
    return max(c for c in range(1, min(tiles, n_steps) + 1) if tiles % c == 0)


def _mixer(x, mem, tables, g_mix, w_in, w_kt, w_pool_bd, pool_scale, w_a, g_ret, b_ret, w_r, g_mem,
           w_mem_kv, w_c, w_out, w_up, w_down):
    b, s, d = x.shape
    m = mem.shape[1]
    ts = min(MIXER_ROWS, s)
    assert s % ts == 0 and ts % RET_CHUNK == 0 and ts % OUT_ROWS == 0
    ns = s // ts
    up_chunks = _cast_chunks(w_up.shape[0], b * ns)
    down_chunks = _cast_chunks(w_down.shape[0], b * ns)

    def cast_spec(w, chunks):
        return pl.BlockSpec((w.shape[0] // chunks, w.shape[1]),
                            lambda bi, si: (jnp.minimum(bi * ns + si, chunks - 1), 0))

    cos2, sin2, cos2t, sin2t, dintra, qdec, kdect, cdec = tables
    row_tile = pl.BlockSpec((None, ts, d), lambda bi, si: (bi, si, 0))
    pos_tile = pl.BlockSpec((ts, RET_QK_DIM), lambda bi, si: (si, 0))
    pos_tile_t = pl.BlockSpec((RET_QK_DIM, ts), lambda bi, si: (0, si))
    in_specs = [row_tile,
                pl.BlockSpec((None, m, d), lambda bi, si: (bi, 0, 0)),
                pos_tile, pos_tile, pos_tile_t, pos_tile_t] + [_vmem_full()] * 17 + [
                cast_spec(w_up, up_chunks), cast_spec(w_down, down_chunks)]
    return pl.pallas_call(
        functools.partial(_mixer_kernel, up_chunks=up_chunks, down_chunks=down_chunks),
        grid=(b, ns),
        in_specs=in_specs,
        out_specs=[row_tile, cast_spec(w_up, up_chunks), cast_spec(w_down, down_chunks)],
        out_shape=[jax.ShapeDtypeStruct((b, s, d), F32),
                   jax.ShapeDtypeStruct(w_up.shape, BF16),
                   jax.ShapeDtypeStruct(w_down.shape, BF16)],
        scratch_shapes=[
            pltpu.VMEM((ts, d), BF16),
            pltpu.VMEM((POOL_HALO, POOL_WIDTH), F32),
            pltpu.VMEM((RET_QK_DIM, RET_V_WIDTH), F32),
            pltpu.VMEM((XA_WIDTH, m), BF16),
            pltpu.VMEM((m, XA_WIDTH), BF16),
            pltpu.VMEM((ts, RET_V_WIDTH), F32),
            pltpu.VMEM((ts, d), F32),
        ],
        compiler_params=pltpu.CompilerParams(
            dimension_semantics=("arbitrary", "arbitrary"),
            vmem_limit_bytes=V7X_VMEM_LIMIT_BYTES),
        name="mixer",
    )(x, mem, cos2, sin2, cos2t, sin2t, dintra, qdec, kdect, cdec,
      g_mix, w_in, w_kt, w_pool_bd, pool_scale, w_a, g_ret, b_ret, w_r, g_mem, w_mem_kv, w_c, w_out,
      w_up, w_down)


def _ffn_kernel(x_ref, g_ffn_ref, w_up_ref, conv_w_ref, conv_b_ref, w_down_ref, g_out_ref,
                o_ref, h_scr, a_scr, carry_scr, g_scr, *, final_norm):
    ts = x_ref.shape[0]

    @pl.when(pl.program_id(1) == 0)
    def _start_of_sequence():
        carry_scr[...] = jnp.zeros(carry_scr.shape, F32)

    h_scr[...] = (_rms(x_ref[...]) * g_ffn_ref[...]).astype(BF16)
    for j in range(FFN_HIDDEN // FFN_COLS):
        cs = slice(j * FFN_COLS, (j + 1) * FFN_COLS)
        a = _dot(h_scr[...], w_up_ref[:, cs])
        gate = _dot(h_scr[...], w_up_ref[:, FFN_HIDDEN + j * FFN_COLS:FFN_HIDDEN + (j + 1) * FFN_COLS])
        buf = a_scr.at[j % 2]
        buf[0:CONV_HALO, :] = carry_scr[:, cs]
        buf[CONV_HALO:CONV_HALO + ts, :] = a
        carry_scr[:, cs] = a[ts - CONV_HALO:, :]
        conv = a * conv_w_ref[CONV_WIDTH - 1:CONV_WIDTH, cs] + conv_b_ref[:, cs]
        for tap in range(CONV_WIDTH - 1):
            back = CONV_WIDTH - 1 - tap
            conv = conv + buf[CONV_HALO - back:CONV_HALO - back + ts, :] * conv_w_ref[tap:tap + 1, cs]
        g_scr[:, cs] = (jax.nn.gelu(conv) * gate).astype(BF16)
    for r0 in range(0, ts, OUT_ROWS):
        rows = slice(r0, r0 + OUT_ROWS)
        y = x_ref[rows, :] + _dot(g_scr[rows, :], w_down_ref[...])
        if final_norm:
            y = _rms(y) * g_out_ref[...]
        o_ref[rows, :] = y


def _ffn(x, g_ffn, w_up, conv_w, conv_b, w_down, g_out, final_norm):
    b, s, d = x.shape
    ts = min(FFN_ROWS, s)
    assert s % ts == 0 and FFN_HIDDEN % FFN_COLS == 0 and ts % OUT_ROWS == 0
    row_tile = pl.BlockSpec((None, ts, d), lambda bi, si: (bi, si, 0))
    return pl.pallas_call(
        functools.partial(_ffn_kernel, final_norm=final_norm),
        grid=(b, s // ts),
        in_specs=[row_tile] + [_vmem_full()] * 6,
        out_specs=row_tile,
        out_shape=jax.ShapeDtypeStruct((b, s, d), F32),
        scratch_shapes=[
            pltpu.VMEM((ts, d), BF16),
            pltpu.VMEM((2, ts + CONV_HALO, FFN_COLS), F32),
            pltpu.VMEM((CONV_HALO, FFN_HIDDEN), F32),
            pltpu.VMEM((ts, FFN_HIDDEN), BF16),
        ],
        compiler_params=pltpu.CompilerParams(
            dimension_semantics=("arbitrary", "arbitrary"),
            vmem_limit_bytes=V7X_VMEM_LIMIT_BYTES),
        name="ffn",
    )(x, g_ffn, w_up, conv_w, conv_b, w_down, g_out)


def _pair_block_diag(w):
    g, c, dd = w.shape
    zero = jnp.zeros((g // 2, c, dd), w.dtype)
    top = jnp.concatenate([w[0::2], zero], axis=2)
    bottom = jnp.concatenate([zero, w[1::2]], axis=2)
    return jnp.concatenate([top, bottom], axis=1)


def kernel(x, mem, g_mix, w_in, w_pool, pool_scale, w_a, g_ret, b_ret, w_r, g_mem, w_mem_kv, w_c,
           w_out, g_ffn, w_up, conv_w, conv_b, w_down, g_final):
    depth = w_in.shape[0]
    tables = _retention_tables(x.shape[1])
    row = lambda a: a.reshape(1, -1)
    for l in range(depth):
        w_in_l = w_in[l].astype(BF16)
        w_kt = w_in_l[:, OFF_K:OFF_K + RET_QK_WIDTH].T
        x, w_up_l, w_down_l = _mixer(
            x, mem, tables, row(g_mix[l]), w_in_l, w_kt,
            _pair_block_diag(w_pool[l]).astype(BF16), row(pool_scale[l]), w_a[l].astype(BF16),
            row(g_ret[l]), row(b_ret[l]), w_r[l].astype(BF16), row(g_mem[l]),
            w_mem_kv[l].astype(BF16), w_c[l].astype(BF16), w_out[l].astype(BF16), w_up[l], w_down[l])
        x = _ffn(x, row(g_ffn[l]), w_up_l, conv_w[l], row(conv_b[l]), w_down_l, row(g_final),
                 final_norm=(l == depth - 1))
    return x
```

```python
import functools

import jax
import jax.numpy as jnp
import numpy as np
from jax import lax
from jax.experimental import pallas as pl
from jax.experimental.pallas import tpu as pltpu

D_MODEL = 1024
POOL_WINDOWS = (2, 4, 8, 16)
POOL_GROUP_DIM = D_MODEL // 8
POOL_WIDTH = len(POOL_WINDOWS) * POOL_GROUP_DIM
POOL_HALO = 16
RET_HEADS = 4
RET_QK_DIM = D_MODEL // 8
RET_V_DIM = 2 * RET_QK_DIM
RET_QK_WIDTH = RET_HEADS * RET_QK_DIM
RET_V_WIDTH = RET_HEADS * RET_V_DIM
RET_CHUNK = 128
ROPE_BASE = 10000.0
XA_HEADS = 4
XA_HEAD_DIM = D_MODEL // 8
XA_WIDTH = XA_HEADS * XA_HEAD_DIM
FFN_HIDDEN = 2816
CONV_WIDTH = 3
CONV_HALO = 8
BF16_ROWS = 16
EPS = 1e-6

OFF_HP = 0
OFF_Q = OFF_HP + POOL_WIDTH
OFF_K = OFF_Q + RET_QK_WIDTH
OFF_V = OFF_K + RET_QK_WIDTH
OFF_GR = OFF_V + RET_V_WIDTH
OFF_QX = OFF_GR + RET_V_WIDTH
OFF_GATE = OFF_QX + XA_WIDTH

MIXER_ROWS = 512
FFN_ROWS = 1024
FFN_COLS = 256
OUT_ROWS = 256
V7X_VMEM_LIMIT_BYTES = 56 * 1024 * 1024

BF16 = jnp.bfloat16
F32 = jnp.float32


def _dot(a, b):
    return jnp.dot(a, b, preferred_element_type=F32)


def _rms(x):
    return x * lax.rsqrt(jnp.mean(x * x, axis=-1, keepdims=True) + EPS)


def _sigmoid(x):
    return 0.5 * jnp.tanh(0.5 * x) + 0.5


def _mixer_kernel(x_ref, mem_ref, cos_ref, sin_ref, cost_ref, sint_ref, dintra_ref, qdec_ref, kdect_ref,
                  cdec_ref, g_mix_ref, w_in_ref, w_pool_ref, pool_scale_ref, w_a_ref, g_ret_ref, b_ret_ref,
                  w_r_ref, g_mem_ref, w_mem_kv_ref, w_c_ref, w_out_ref, w_up_ref, w_down_ref,
                  o_ref, w_up_bf_ref, w_down_bf_ref,
                  h_scr, w_kt_scr, e_scr, r_scr, kt_scr, v_scr, ret_scr, m_scr, *, up_chunks, down_chunks):
    ts = x_ref.shape[0]
    s_idx = pl.program_id(1)
    step = pl.program_id(0) * pl.num_programs(1) + s_idx

    @pl.when(step < up_chunks)
    def _cast_up():
        w_up_bf_ref[...] = w_up_ref[...].astype(BF16)

    @pl.when(step < down_chunks)
    def _cast_down():
        w_down_bf_ref[...] = w_down_ref[...].astype(BF16)

    @pl.when(step == 0)
    def _transpose_key_projection():
        for hh in range(RET_HEADS):
            cols = slice(OFF_K + hh * RET_QK_DIM, OFF_K + (hh + 1) * RET_QK_DIM)
            w_kt_scr[hh * RET_QK_DIM:(hh + 1) * RET_QK_DIM, :] = w_in_ref[:, cols].astype(F32).T.astype(BF16)

    @pl.when(s_idx == 0)
    def _start_of_sequence():
        e_scr[...] = jnp.zeros(e_scr.shape, F32)
        r_scr[...] = jnp.zeros(r_scr.shape, F32)
        mem_n = (_rms(mem_ref[...]) * g_mem_ref[...]).astype(BF16)
        kv = _dot(mem_n, w_mem_kv_ref[...])
        kt_scr[...] = kv[:, :XA_WIDTH].T.astype(BF16)
        v_scr[...] = kv[:, XA_WIDTH:].astype(BF16)

    h_scr[...] = (_rms(x_ref[...]) * g_mix_ref[...]).astype(BF16)

    def proj(off, width):
        return _dot(h_scr[...], w_in_ref[:, off:off + width])

    def rope(a):
        cos = cos_ref[...]
        sin = sin_ref[...]
        parts = []
        for hh in range(RET_HEADS):
            ah = a[:, hh * RET_QK_DIM:(hh + 1) * RET_QK_DIM]
            parts.append(ah * cos + pltpu.roll(ah, RET_QK_DIM // 2, 1) * sin)
        return jnp.concatenate(parts, axis=-1)

    def qslice(hh):
        return slice(hh * RET_QK_DIM, (hh + 1) * RET_QK_DIM)

    def vslice(hh):
        return slice(hh * RET_V_DIM, (hh + 1) * RET_V_DIM)

    def retention_chunk(c, qd, kdt, v):
        rows = slice(c * RET_CHUNK, (c + 1) * RET_CHUNK)
        zero = jnp.zeros((RET_QK_DIM, RET_QK_DIM), BF16)
        sc = []
        for h0 in range(0, RET_HEADS, 2):
            keys = jnp.concatenate([jnp.concatenate([kdt[c][h0], zero], axis=1),
                                    jnp.concatenate([zero, kdt[c][h0 + 1]], axis=1)], axis=0)
            pair = _dot(qd[rows, h0 * RET_QK_DIM:(h0 + 2) * RET_QK_DIM], keys)
            sc += [pair[:, :RET_QK_DIM], pair[:, RET_QK_DIM:]]
        state = [r_scr[:, vslice(hh)] for hh in range(RET_HEADS)]
        upd = [_dot(kdt[c][hh], v[rows, vslice(hh)]) for hh in range(RET_HEADS)]
        for hh in range(RET_HEADS):
            s_h = (sc[hh] * dintra_ref[hh]).astype(BF16)
            lhs = jnp.concatenate([s_h, qd[rows, qslice(hh)]], axis=1)
            rhs = jnp.concatenate([v[rows, vslice(hh)], state[hh].astype(BF16)], axis=0)
            ret_scr[rows, vslice(hh)] = _dot(lhs, rhs)
            r_scr[:, vslice(hh)] = cdec_ref[:, vslice(hh)] * state[hh] + upd[hh]

    n_chunks = ts // RET_CHUNK

    hp = proj(OFF_HP, POOL_WIDTH)
    q = proj(OFF_Q, RET_QK_WIDTH)
    kt = lax.dot_general(w_kt_scr[...], h_scr[...], (((1,), (1,)), ((), ())), preferred_element_type=F32)

    t1 = (lax.broadcasted_iota(jnp.int32, (ts, POOL_GROUP_DIM), 0) + (s_idx * ts + 1)).astype(F32)
    pooled = []
    for gi, w in enumerate(POOL_WINDOWS):
        cols = slice(gi * POOL_GROUP_DIM, (gi + 1) * POOL_GROUP_DIM)
        cur = hp[:, cols]
        acc = jnp.concatenate([e_scr[:, cols], cur], axis=0)
        span = 1
        while span < w:
            acc = acc + pltpu.roll(acc, span, 0)
            span *= 2
        pooled.append(acc[POOL_HALO:, :] / jnp.minimum(t1, float(w)) - cur)
    e_scr[...] = hp[ts - POOL_HALO:, :]
    pooled = jnp.concatenate(pooled, axis=-1).astype(BF16)

    v = proj(OFF_V, RET_V_WIDTH).astype(BF16)

    q = rope(q)
    qd = jnp.concatenate([(q[c * RET_CHUNK:(c + 1) * RET_CHUNK, :] * qdec_ref[...]).astype(BF16)
                          for c in range(n_chunks)], axis=0)
    half = RET_QK_DIM // 2
    kdt = [[] for _ in range(n_chunks)]
    for hh in range(RET_HEADS):
        kh = kt[qslice(hh), :]
        swapped = jnp.concatenate([kh[half:, :], kh[:half, :]], axis=0)
        kh = (kh * cost_ref[...] + swapped * sint_ref[...]) * (RET_QK_DIM ** -0.5)
        for c in range(n_chunks):
            kdt[c].append((kh[:, c * RET_CHUNK:(c + 1) * RET_CHUNK] * kdect_ref[qslice(hh), :]).astype(BF16))

    pair_w = 2 * POOL_GROUP_DIM
    y_pool = jnp.concatenate(
        [_dot(pooled[:, i * pair_w:(i + 1) * pair_w], w_pool_ref[i]) for i in range(POOL_WIDTH // pair_w)],
        axis=-1) * pool_scale_ref[...]
    qx = proj(OFF_QX, XA_WIDTH).astype(BF16)
    y_pool = _dot(y_pool.astype(BF16), w_a_ref[...])

    retention_chunk(0, qd, kdt, v)
    m_scr[...] = _sigmoid(proj(OFF_GATE, D_MODEL)) * y_pool

    probs, denom = [], []
    for hh in range(XA_HEADS):
        hs = qslice(hh)
        sc = _dot(qx[:, hs], kt_scr[hs, :]) * (XA_HEAD_DIM ** -0.5)
        p = jnp.exp(sc - jnp.max(sc, axis=-1, keepdims=True))
        probs.append(p.astype(BF16))
        denom.append(jnp.sum(p, axis=-1, keepdims=True))

    gr = proj(OFF_GR, RET_V_WIDTH)
    for c in range(1, n_chunks):
        retention_chunk(c, qd, kdt, v)

    heads = [_dot(probs[hh], v_scr[:, qslice(hh)]) / denom[hh] for hh in range(XA_HEADS)]
    gate_mem = _sigmoid(proj(OFF_GATE + 2 * D_MODEL, D_MODEL))
    y_mem = _dot(jnp.concatenate(heads, axis=-1).astype(BF16), w_c_ref[...])
    m_scr[...] += gate_mem * y_mem

    normed = []
    for hh in range(RET_HEADS):
        oh = ret_scr[:, vslice(hh)]
        dev = oh - jnp.mean(oh, axis=-1, keepdims=True)
        normed.append(dev * lax.rsqrt(jnp.mean(dev * dev, axis=-1, keepdims=True) + EPS))
    o = jnp.concatenate(normed, axis=-1) * g_ret_ref[...] + b_ret_ref[...]
    gate_ret = _sigmoid(proj(OFF_GATE + D_MODEL, D_MODEL))
    y_ret = _dot((gr * _sigmoid(gr) * o).astype(BF16), w_r_ref[...])
    merged = (m_scr[...] + gate_ret * y_ret).astype(BF16)

    for r0 in range(0, ts, OUT_ROWS):
        rows = slice(r0, r0 + OUT_ROWS)
        o_ref[rows, :] = x_ref[rows, :] + _dot(merged[rows, :], w_out_ref[...])


def _retention_tables(seq):
    half = RET_QK_DIM // 2
    pos = np.arange(seq, dtype=np.float64)
    inv = ROPE_BASE ** (-np.arange(half, dtype=np.float64) / half)
    ang = pos[:, None] * inv[None, :]
    cos, sin = np.cos(ang), np.sin(ang)
    cos2 = np.concatenate([cos, cos], axis=-1)
    sin2 = np.concatenate([-sin, sin], axis=-1)
    c = RET_CHUNK
    log_gamma = np.log1p(-np.exp2(-5.0 - np.arange(RET_HEADS, dtype=np.float64)))
    lg = log_gamma[:, None, None]
    idx = np.arange(c, dtype=np.float64)
    rel = idx[:, None] - idx[None, :]
    q_decay = np.exp((idx + 1.0)[None, :, None] * lg)
    k_decay = np.exp((c - 1.0 - idx)[None, :, None] * lg)
    chunk_decay = np.exp(c * lg)
    decay_intra = np.where(rel >= 0, 1.0 / chunk_decay, 0.0)

    def lanes(a, width):
        r = a.shape[1]
        return np.broadcast_to(a.transpose(1, 0, 2), (r, RET_HEADS, width)).reshape(r, RET_HEADS * width)

    k_decay_t = np.broadcast_to(k_decay.transpose(0, 2, 1), (RET_HEADS, RET_QK_DIM, c))
    tables = (cos2, sin2, cos2.T, sin2.T, decay_intra, lanes(q_decay, RET_QK_DIM),
              k_decay_t.reshape(RET_QK_WIDTH, c), lanes(chunk_decay, RET_V_DIM))
    return tuple(jnp.asarray(np.ascontiguousarray(t), F32) for t in tables)


def _vmem_full():
    return pl.BlockSpec(memory_space=pltpu.VMEM)


def _cast_chunks(rows, n_steps):
    tiles = rows // BF16_ROWS
    assert rows % BF16_ROWS == 0
    return max(c for c in range(1, min(tiles, n_steps) + 1) if tiles % c == 0)


def _mixer(x, mem, tables, g_mix, w_in, w_pool_bd, pool_scale, w_a, g_ret, b_ret, w_r, g_mem,
           w_mem_kv, w_c, w_out, w_up, w_down):
    b, s, d = x.shape
    m = mem.shape[1]
    ts = min(MIXER_ROWS, s)
    assert s % ts == 0 and ts % RET_CHUNK == 0 and ts % OUT_ROWS == 0
    ns = s // ts
    up_chunks = _cast_chunks(w_up.shape[0], b * ns)
    down_chunks = _cast_chunks(w_down.shape[0], b * ns)

    def cast_spec(w, chunks):
        return pl.BlockSpec((w.shape[0] // chunks, w.shape[1]),
                            lambda bi, si: (jnp.minimum(bi * ns + si, chunks - 1), 0))

    cos2, sin2, cos2t, sin2t, dintra, qdec, kdect, cdec = tables
    row_tile = pl.BlockSpec((None, ts, d), lambda bi, si: (bi, si, 0))
    pos_tile = pl.BlockSpec((ts, RET_QK_DIM), lambda bi, si: (si, 0))
    pos_tile_t = pl.BlockSpec((RET_QK_DIM, ts), lambda bi, si: (0, si))
    in_specs = [row_tile,
                pl.BlockSpec((None, m, d), lambda bi, si: (bi, 0, 0)),
                pos_tile, pos_tile, pos_tile_t, pos_tile_t] + [_vmem_full()] * 16 + [
                cast_spec(w_up, up_chunks), cast_spec(w_down, down_chunks)]
    return pl.pallas_call(
        functools.partial(_mixer_kernel, up_chunks=up_chunks, down_chunks=down_chunks),
        grid=(b, ns),
        in_specs=in_specs,
        out_specs=[row_tile, cast_spec(w_up, up_chunks), cast_spec(w_down, down_chunks)],
        out_shape=[jax.ShapeDtypeStruct((b, s, d), F32),
                   jax.ShapeDtypeStruct(w_up.shape, BF16),
                   jax.ShapeDtypeStruct(w_down.shape, BF16)],
        scratch_shapes=[
            pltpu.VMEM((ts, d), BF16),
            pltpu.VMEM((RET_QK_WIDTH, d), BF16),
            pltpu.VMEM((POOL_HALO, POOL_WIDTH), F32),
            pltpu.VMEM((RET_QK_DIM, RET_V_WIDTH), F32),
            pltpu.VMEM((XA_WIDTH, m), BF16),
            pltpu.VMEM((m, XA_WIDTH), BF16),
            pltpu.VMEM((ts, RET_V_WIDTH), F32),
            pltpu.VMEM((ts, d), F32),
        ],
        compiler_params=pltpu.CompilerParams(
            dimension_semantics=("arbitrary", "arbitrary"),
            vmem_limit_bytes=V7X_VMEM_LIMIT_BYTES),
        name="mixer",
    )(x, mem, cos2, sin2, cos2t, sin2t, dintra, qdec, kdect, cdec,
      g_mix, w_in, w_pool_bd, pool_scale, w_a, g_ret, b_ret, w_r, g_mem, w_mem_kv, w_c, w_out,
      w_up, w_down)


def _ffn_kernel(x_ref, g_ffn_ref, w_up_ref, conv_w_ref, conv_b_ref, w_down_ref, g_out_ref,
                o_ref, h_scr, a_scr, carry_scr, g_scr, *, final_norm):
    ts = x_ref.shape[0]

    @pl.when(pl.program_id(1) == 0)
    def _start_of_sequence():
        carry_scr[...] = jnp.zeros(carry_scr.shape, F32)

    h_scr[...] = (_rms(x_ref[...]) * g_ffn_ref[...]).astype(BF16)
    for j in range(FFN_HIDDEN // FFN_COLS):
        cs = slice(j * FFN_COLS, (j + 1) * FFN_COLS)
        a = _dot(h_scr[...], w_up_ref[:, cs])
        gate = _dot(h_scr[...], w_up_ref[:, FFN_HIDDEN + j * FFN_COLS:FFN_HIDDEN + (j + 1) * FFN_COLS])
        buf = a_scr.at[j % 2]
        buf[0:CONV_HALO, :] = carry_scr[:, cs]
        buf[CONV_HALO:CONV_HALO + ts, :] = a
        carry_scr[:, cs] = a[ts - CONV_HALO:, :]
        conv = a * conv_w_ref[CONV_WIDTH - 1:CONV_WIDTH, cs] + conv_b_ref[:, cs]
        for tap in range(CONV_WIDTH - 1):
            back = CONV_WIDTH - 1 - tap
            conv = conv + buf[CONV_HALO - back:CONV_HALO - back + ts, :] * conv_w_ref[tap:tap + 1, cs]
        g_scr[:, cs] = (jax.nn.gelu(conv) * gate).astype(BF16)
    for r0 in range(0, ts, OUT_ROWS):
        rows = slice(r0, r0 + OUT_ROWS)
        y = x_ref[rows, :] + _dot(g_scr[rows, :], w_down_ref[...])
        if final_norm:
            y = _rms(y) * g_out_ref[...]
        o_ref[rows, :] = y


def _ffn(x, g_ffn, w_up, conv_w, conv_b, w_down, g_out, final_norm):
    b, s, d = x.shape
    ts = min(FFN_ROWS, s)
    assert s % ts == 0 and FFN_HIDDEN % FFN_COLS == 0 and ts % OUT_ROWS == 0
    row_tile = pl.BlockSpec((None, ts, d), lambda bi, si: (bi, si, 0))
    return pl.pallas_call(
        functools.partial(_ffn_kernel, final_norm=final_norm),
        grid=(b, s // ts),
        in_specs=[row_tile] + [_vmem_full()] * 6,
        out_specs=row_tile,
        out_shape=jax.ShapeDtypeStruct((b, s, d), F32),
        scratch_shapes=[
            pltpu.VMEM((ts, d), BF16),
            pltpu.VMEM((2, ts + CONV_HALO, FFN_COLS), F32),
            pltpu.VMEM((CONV_HALO, FFN_HIDDEN), F32),
            pltpu.VMEM((ts, FFN_HIDDEN), BF16),
        ],
        compiler_params=pltpu.CompilerParams(
            dimension_semantics=("arbitrary", "arbitrary"),
            vmem_limit_bytes=V7X_VMEM_LIMIT_BYTES),
        name="ffn",
    )(x, g_ffn, w_up, conv_w, conv_b, w_down, g_out)


def _pair_block_diag(w):
    g, c, dd = w.shape
    zero = jnp.zeros((g // 2, c, dd), w.dtype)
    top = jnp.concatenate([w[0::2], zero], axis=2)
    bottom = jnp.concatenate([zero, w[1::2]], axis=2)
    return jnp.concatenate([top, bottom], axis=1)


def kernel(x, mem, g_mix, w_in, w_pool, pool_scale, w_a, g_ret, b_ret, w_r, g_mem, w_mem_kv, w_c,
           w_out, g_ffn, w_up, conv_w, conv_b, w_down, g_final):
    depth = w_in.shape[0]
    tables = _retention_tables(x.shape[1])
    row = lambda a: a.reshape(1, -1)
    for l in range(depth):
        x, w_up_l, w_down_l = _mixer(
            x, mem, tables, row(g_mix[l]), w_in[l].astype(BF16),
            _pair_block_diag(w_pool[l]).astype(BF16), row(pool_scale[l]), w_a[l].astype(BF16),
            row(g_ret[l]), row(b_ret[l]), w_r[l].astype(BF16), row(g_mem[l]),
            w_mem_kv[l].astype(BF16), w_c[l].astype(BF16), w_out[l].astype(BF16), w_up[l], w_down[l])
        x = _ffn(x, row(g_ffn[l]), w_up_l, conv_w[l], row(conv_b[l]), w_down_l, row(g_final),
                 final_norm=(l == depth - 1))
    return x
```

```python
import functools

import jax
import jax.numpy as jnp
import numpy as np
from jax import lax
from jax.experimental import pallas as pl
from jax.experimental.pallas import tpu as pltpu

D_MODEL = 1024
POOL_WINDOWS = (2, 4, 8, 16)
POOL_GROUP_DIM = D_MODEL // 8
POOL_WIDTH = len(POOL_WINDOWS) * POOL_GROUP_DIM
POOL_HALO = 16
RET_HEADS = 4
RET_QK_DIM = D_MODEL // 8
RET_V_DIM = 2 * RET_QK_DIM
RET_QK_WIDTH = RET_HEADS * RET_QK_DIM
RET_V_WIDTH = RET_HEADS * RET_V_DIM
RET_CHUNK = 128
ROPE_BASE = 10000.0
XA_HEADS = 4
XA_HEAD_DIM = D_MODEL // 8
XA_WIDTH = XA_HEADS * XA_HEAD_DIM
FFN_HIDDEN = 2816
CONV_WIDTH = 3
CONV_HALO = 8
BF16_ROWS = 16
EPS = 1e-6

OFF_HP = 0
OFF_Q = OFF_HP + POOL_WIDTH
OFF_K = OFF_Q + RET_QK_WIDTH
OFF_V = OFF_K + RET_QK_WIDTH
OFF_GR = OFF_V + RET_V_WIDTH
OFF_QX = OFF_GR + RET_V_WIDTH
OFF_GATE = OFF_QX + XA_WIDTH

MIXER_ROWS = 512
FFN_ROWS = 1024
FFN_COLS = 256
OUT_ROWS = 256
V7X_VMEM_LIMIT_BYTES = 56 * 1024 * 1024

BF16 = jnp.bfloat16
F32 = jnp.float32


def _dot(a, b):
    return jnp.dot(a, b, preferred_element_type=F32)


def _rms(x):
    return x * lax.rsqrt(jnp.mean(x * x, axis=-1, keepdims=True) + EPS)


def _sigmoid(x):
    return 0.5 * jnp.tanh(0.5 * x) + 0.5


def _mixer_kernel(x_ref, mem_ref, cos_ref, sin_ref, cost_ref, sint_ref, dintra_ref, qdec_ref, kdect_ref,
                  cdec_ref, g_mix_ref, w_in_ref, w_pool_ref, pool_scale_ref, w_a_ref, g_ret_ref, b_ret_ref,
                  w_r_ref, g_mem_ref, w_mem_kv_ref, w_c_ref, w_out_ref, w_up_ref, w_down_ref,
                  o_ref, w_up_bf_ref, w_down_bf_ref,
                  h_scr, w_kt_scr, e_scr, r_scr, kt_scr, v_scr, kd_scr, gn_scr, m_scr, *, up_chunks, down_chunks):
    ts = x_ref.shape[0]
    s_idx = pl.program_id(1)
    step = pl.program_id(0) * pl.num_programs(1) + s_idx

    @pl.when(step < up_chunks)
    def _cast_up():
        w_up_bf_ref[...] = w_up_ref[...].astype(BF16)

    @pl.when(step < down_chunks)
    def _cast_down():
        w_down_bf_ref[...] = w_down_ref[...].astype(BF16)

    @pl.when(step == 0)
    def _transpose_key_projection():
        for hh in range(RET_HEADS):
            cols = slice(OFF_K + hh * RET_QK_DIM, OFF_K + (hh + 1) * RET_QK_DIM)
            w_kt_scr[hh * RET_QK_DIM:(hh + 1) * RET_QK_DIM, :] = w_in_ref[:, cols].astype(F32).T.astype(BF16)

    @pl.when(s_idx == 0)
    def _start_of_sequence():
        e_scr[...] = jnp.zeros(e_scr.shape, F32)
        r_scr[...] = jnp.zeros(r_scr.shape, F32)
        mem_n = (_rms(mem_ref[...]) * g_mem_ref[...]).astype(BF16)
        kv = _dot(mem_n, w_mem_kv_ref[...])
        kt_scr[...] = kv[:, :XA_WIDTH].T.astype(BF16)
        v_scr[...] = kv[:, XA_WIDTH:].astype(BF16)

    h_scr[...] = (_rms(x_ref[...]) * g_mix_ref[...]).astype(BF16)

    def proj(off, width):
        return _dot(h_scr[...], w_in_ref[:, off:off + width])

    def rope(a):
        cos = cos_ref[...]
        sin = sin_ref[...]
        parts = []
        for hh in range(RET_HEADS):
            ah = a[:, hh * RET_QK_DIM:(hh + 1) * RET_QK_DIM]
            parts.append(ah * cos + pltpu.roll(ah, RET_QK_DIM // 2, 1) * sin)
        return jnp.concatenate(parts, axis=-1)

    def qslice(hh):
        return slice(hh * RET_QK_DIM, (hh + 1) * RET_QK_DIM)

    def vslice(hh):
        return slice(hh * RET_V_DIM, (hh + 1) * RET_V_DIM)

    def retention_chunk(c, qd, v, gr):
        rows = slice(c * RET_CHUNK, (c + 1) * RET_CHUNK)
        zero = jnp.zeros((RET_QK_DIM, RET_QK_DIM), BF16)
        kdt = [kd_scr[qslice(hh), rows] for hh in range(RET_HEADS)]
        sc = []
        for h0 in range(0, RET_HEADS, 2):
            keys = jnp.concatenate([jnp.concatenate([kdt[h0], zero], axis=1),
                                    jnp.concatenate([zero, kdt[h0 + 1]], axis=1)], axis=0)
            pair = _dot(qd[rows, h0 * RET_QK_DIM:(h0 + 2) * RET_QK_DIM], keys)
            sc += [pair[:, :RET_QK_DIM], pair[:, RET_QK_DIM:]]
        state = [r_scr[:, vslice(hh)] for hh in range(RET_HEADS)]
        upd = [_dot(kdt[hh], v[rows, vslice(hh)]) for hh in range(RET_HEADS)]
        for hh in range(RET_HEADS):
            s_h = (sc[hh] * dintra_ref[hh]).astype(BF16)
            lhs = jnp.concatenate([s_h, qd[rows, qslice(hh)]], axis=1)
            rhs = jnp.concatenate([v[rows, vslice(hh)], state[hh].astype(BF16)], axis=0)
            oh = _dot(lhs, rhs)
            r_scr[:, vslice(hh)] = cdec_ref[:, vslice(hh)] * state[hh] + upd[hh]
            dev = oh - jnp.mean(oh, axis=-1, keepdims=True)
            oh = dev * lax.rsqrt(jnp.mean(dev * dev, axis=-1, keepdims=True) + EPS)
            oh = oh * g_ret_ref[:, vslice(hh)] + b_ret_ref[:, vslice(hh)]
            gh = gr[rows, vslice(hh)]
            gn_scr[rows, vslice(hh)] = (gh * _sigmoid(gh) * oh).astype(BF16)

    n_chunks = ts // RET_CHUNK

    hp = proj(OFF_HP, POOL_WIDTH)
    q = proj(OFF_Q, RET_QK_WIDTH)
    kt = lax.dot_general(w_kt_scr[...], h_scr[...], (((1,), (1,)), ((), ())), preferred_element_type=F32)

    t1 = (lax.broadcasted_iota(jnp.int32, (ts, POOL_GROUP_DIM), 0) + (s_idx * ts + 1)).astype(F32)
    pooled = []
    for gi, w in enumerate(POOL_WINDOWS):
        cols = slice(gi * POOL_GROUP_DIM, (gi + 1) * POOL_GROUP_DIM)
        cur = hp[:, cols]
        acc = jnp.concatenate([e_scr[:, cols], cur], axis=0)
        span = 1
        while span < w:
            acc = acc + pltpu.roll(acc, span, 0)
            span *= 2
        pooled.append(acc[POOL_HALO:, :] / jnp.minimum(t1, float(w)) - cur)
    e_scr[...] = hp[ts - POOL_HALO:, :]
    pooled = jnp.concatenate(pooled, axis=-1).astype(BF16)

    v = proj(OFF_V, RET_V_WIDTH).astype(BF16)

    q = rope(q)
    qd = jnp.concatenate([(q[c * RET_CHUNK:(c + 1) * RET_CHUNK, :] * qdec_ref[...]).astype(BF16)
                          for c in range(n_chunks)], axis=0)
    half = RET_QK_DIM // 2
    for hh in range(RET_HEADS):
        kh = kt[qslice(hh), :]
        swapped = jnp.concatenate([kh[half:, :], kh[:half, :]], axis=0)
        kh = (kh * cost_ref[...] + swapped * sint_ref[...]) * (RET_QK_DIM ** -0.5)
        for c in range(n_chunks):
            cols = slice(c * RET_CHUNK, (c + 1) * RET_CHUNK)
            kd_scr[qslice(hh), cols] = (kh[:, cols] * kdect_ref[qslice(hh), :]).astype(BF16)

    pair_w = 2 * POOL_GROUP_DIM
    y_pool = jnp.concatenate(
        [_dot(pooled[:, i * pair_w:(i + 1) * pair_w], w_pool_ref[i]) for i in range(POOL_WIDTH // pair_w)],
        axis=-1) * pool_scale_ref[...]
    qx = proj(OFF_QX, XA_WIDTH).astype(BF16)
    y_pool = _dot(y_pool.astype(BF16), w_a_ref[...])

    m_scr[...] = _sigmoid(proj(OFF_GATE, D_MODEL)) * y_pool

    probs, denom = [], []
    for hh in range(XA_HEADS):
        hs = qslice(hh)
        sc = _dot(qx[:, hs], kt_scr[hs, :]) * (XA_HEAD_DIM ** -0.5)
        p = jnp.exp(sc - jnp.max(sc, axis=-1, keepdims=True))
        probs.append(p.astype(BF16))
        denom.append(jnp.sum(p, axis=-1, keepdims=True))

    gr = proj(OFF_GR, RET_V_WIDTH)

    mem_parts = {}

    def attend():
        heads = [_dot(probs[hh], v_scr[:, qslice(hh)]) / denom[hh] for hh in range(XA_HEADS)]
        mem_parts["heads"] = jnp.concatenate(heads, axis=-1).astype(BF16)

    def gate_mem():
        mem_parts["gate"] = _sigmoid(proj(OFF_GATE + 2 * D_MODEL, D_MODEL))

    def add_mem():
        m_scr[...] += mem_parts["gate"] * _dot(mem_parts["heads"], w_c_ref[...])

    fills = [attend, gate_mem, add_mem]
    for c in range(n_chunks):
        retention_chunk(c, qd, v, gr)
        for fill in (fills[c:c + 1] if c < n_chunks - 1 else fills[c:]):
            fill()

    gate_ret = _sigmoid(proj(OFF_GATE + D_MODEL, D_MODEL))
    y_ret = _dot(gn_scr[...], w_r_ref[...])
    merged = (m_scr[...] + gate_ret * y_ret).astype(BF16)

    for r0 in range(0, ts, OUT_ROWS):
        rows = slice(r0, r0 + OUT_ROWS)
        o_ref[rows, :] = x_ref[rows, :] + _dot(merged[rows, :], w_out_ref[...])


def _retention_tables(seq):
    half = RET_QK_DIM // 2
    pos = np.arange(seq, dtype=np.float64)
    inv = ROPE_BASE ** (-np.arange(half, dtype=np.float64) / half)
    ang = pos[:, None] * inv[None, :]
    cos, sin = np.cos(ang), np.sin(ang)
    cos2 = np.concatenate([cos, cos], axis=-1)
    sin2 = np.concatenate([-sin, sin], axis=-1)
    c = RET_CHUNK
    log_gamma = np.log1p(-np.exp2(-5.0 - np.arange(RET_HEADS, dtype=np.float64)))
    lg = log_gamma[:, None, None]
    idx = np.arange(c, dtype=np.float64)
    rel = idx[:, None] - idx[None, :]
    q_decay = np.exp((idx + 1.0)[None, :, None] * lg)
    k_decay = np.exp((c - 1.0 - idx)[None, :, None] * lg)
    chunk_decay = np.exp(c * lg)
    decay_intra = np.where(rel >= 0, 1.0 / chunk_decay, 0.0)

    def lanes(a, width):
        r = a.shape[1]
        return np.broadcast_to(a.transpose(1, 0, 2), (r, RET_HEADS, width)).reshape(r, RET_HEADS * width)

    k_decay_t = np.broadcast_to(k_decay.transpose(0, 2, 1), (RET_HEADS, RET_QK_DIM, c))
    tables = (cos2, sin2, cos2.T, sin2.T, decay_intra, lanes(q_decay, RET_QK_DIM),
              k_decay_t.reshape(RET_QK_WIDTH, c), lanes(chunk_decay, RET_V_DIM))
    return tuple(jnp.asarray(np.ascontiguousarray(t), F32) for t in tables)


def _vmem_full():
    return pl.BlockSpec(memory_space=pltpu.VMEM)


def _cast_chunks(rows, n_steps):
    tiles = rows // BF16_ROWS
    assert rows % BF16_ROWS == 0
    return max(c for c in range(1, min(tiles, n_steps) + 1) if tiles % c == 0)


def _mixer(x, mem, tables, g_mix, w_in, w_pool_bd, pool_scale, w_a, g_ret, b_ret, w_r, g_mem,
           w_mem_kv, w_c, w_out, w_up, w_down):
    b, s, d = x.shape
    m = mem.shape[1]
    ts = min(MIXER_ROWS, s)
    assert s % ts == 0 and ts % RET_CHUNK == 0 and ts % OUT_ROWS == 0
    ns = s // ts
    up_chunks = _cast_chunks(w_up.shape[0], b * ns)
    down_chunks = _cast_chunks(w_down.shape[0], b * ns)

    def cast_spec(w, chunks):
        return pl.BlockSpec((w.shape[0] // chunks, w.shape[1]),
                            lambda bi, si: (jnp.minimum(bi * ns + si, chunks - 1), 0))

    cos2, sin2, cos2t, sin2t, dintra, qdec, kdect, cdec = tables
    row_tile = pl.BlockSpec((None, ts, d), lambda bi, si: (bi, si, 0))
    pos_tile = pl.BlockSpec((ts, RET_QK_DIM), lambda bi, si: (si, 0))
    pos_tile_t = pl.BlockSpec((RET_QK_DIM, ts), lambda bi, si: (0, si))
    in_specs = [row_tile,
                pl.BlockSpec((None, m, d), lambda bi, si: (bi, 0, 0)),
                pos_tile, pos_tile, pos_tile_t, pos_tile_t] + [_vmem_full()] * 16 + [
                cast_spec(w_up, up_chunks), cast_spec(w_down, down_chunks)]
    return pl.pallas_call(
        functools.partial(_mixer_kernel, up_chunks=up_chunks, down_chunks=down_chunks),
        grid=(b, ns),
        in_specs=in_specs,
        out_specs=[row_tile, cast_spec(w_up, up_chunks), cast_spec(w_down, down_chunks)],
        out_shape=[jax.ShapeDtypeStruct((b, s, d), F32),
                   jax.ShapeDtypeStruct(w_up.shape, BF16),
                   jax.ShapeDtypeStruct(w_down.shape, BF16)],
        scratch_shapes=[
            pltpu.VMEM((ts, d), BF16),
            pltpu.VMEM((RET_QK_WIDTH, d), BF16),
            pltpu.VMEM((POOL_HALO, POOL_WIDTH), F32),
            pltpu.VMEM((RET_QK_DIM, RET_V_WIDTH), F32),
            pltpu.VMEM((XA_WIDTH, m), BF16),
            pltpu.VMEM((m, XA_WIDTH), BF16),
            pltpu.VMEM((RET_QK_WIDTH, ts), BF16),
            pltpu.VMEM((ts, RET_V_WIDTH), BF16),
            pltpu.VMEM((ts, d), F32),
        ],
        compiler_params=pltpu.CompilerParams(
            dimension_semantics=("arbitrary", "arbitrary"),
            vmem_limit_bytes=V7X_VMEM_LIMIT_BYTES),
        name="mixer",
    )(x, mem, cos2, sin2, cos2t, sin2t, dintra, qdec, kdect, cdec,
      g_mix, w_in, w_pool_bd, pool_scale, w_a, g_ret, b_ret, w_r, g_mem, w_mem_kv, w_c, w_out,
      w_up, w_down)


def _ffn_kernel(x_ref, g_ffn_ref, w_up_ref, conv_w_ref, conv_b_ref, w_down_ref, g_out_ref,
                o_ref, h_scr, a_scr, carry_scr, g_scr, *, final_norm):
    ts = x_ref.shape[0]

    @pl.when(pl.program_id(1) == 0)
    def _start_of_sequence():
        carry_scr[...] = jnp.zeros(carry_scr.shape, F32)

    h_scr[...] = (_rms(x_ref[...]) * g_ffn_ref[...]).astype(BF16)
    for j in range(FFN_HIDDEN // FFN_COLS):
        cs = slice(j * FFN_COLS, (j + 1) * FFN_COLS)
        a = _dot(h_scr[...], w_up_ref[:, cs])
        gate = _dot(h_scr[...], w_up_ref[:, FFN_HIDDEN + j * FFN_COLS:FFN_HIDDEN + (j + 1) * FFN_COLS])
        buf = a_scr.at[j % 2]
        buf[0:CONV_HALO, :] = carry_scr[:, cs]
        buf[CONV_HALO:CONV_HALO + ts, :] = a
        carry_scr[:, cs] = a[ts - CONV_HALO:, :]
        conv = a * conv_w_ref[CONV_WIDTH - 1:CONV_WIDTH, cs] + conv_b_ref[:, cs]
        for tap in range(CONV_WIDTH - 1):
            back = CONV_WIDTH - 1 - tap
            conv = conv + buf[CONV_HALO - back:CONV_HALO - back + ts, :] * conv_w_ref[tap:tap + 1, cs]
        g_scr[:, cs] = (jax.nn.gelu(conv) * gate).astype(BF16)
    for r0 in range(0, ts, OUT_ROWS):
        rows = slice(r0, r0 + OUT_ROWS)
        y = x_ref[rows, :] + _dot(g_scr[rows, :], w_down_ref[...])
        if final_norm:
            y = _rms(y) * g_out_ref[...]
        o_ref[rows, :] = y


def _ffn(x, g_ffn, w_up, conv_w, conv_b, w_down, g_out, final_norm):
    b, s, d = x.shape
    ts = min(FFN_ROWS, s)
    assert s % ts == 0 and FFN_HIDDEN % FFN_COLS == 0 and ts % OUT_ROWS == 0
    row_tile = pl.BlockSpec((None, ts, d), lambda bi, si: (bi, si, 0))
    return pl.pallas_call(
        functools.partial(_ffn_kernel, final_norm=final_norm),
        grid=(b, s // ts),
        in_specs=[row_tile] + [_vmem_full()] * 6,
        out_specs=row_tile,
        out_shape=jax.ShapeDtypeStruct((b, s, d), F32),
        scratch_shapes=[
            pltpu.VMEM((ts, d), BF16),
            pltpu.VMEM((2, ts + CONV_HALO, FFN_COLS), F32),
            pltpu.VMEM((CONV_HALO, FFN_HIDDEN), F32),
            pltpu.VMEM((ts, FFN_HIDDEN), BF16),
        ],
        compiler_params=pltpu.CompilerParams(
            dimension_semantics=("arbitrary", "arbitrary"),
            vmem_limit_bytes=V7X_VMEM_LIMIT_BYTES),
        name="ffn",
    )(x, g_ffn, w_up, conv_w, conv_b, w_down, g_out)


def _pair_block_diag(w):
    g, c, dd = w.shape
    zero = jnp.zeros((g // 2, c, dd), w.dtype)
    top = jnp.concatenate([w[0::2], zero], axis=2)
    bottom = jnp.concatenate([zero, w[1::2]], axis=2)
    return jnp.concatenate([top, bottom], axis=1)


def kernel(x, mem, g_mix, w_in, w_pool, pool_scale, w_a, g_ret, b_ret, w_r, g_mem, w_mem_kv, w_c,
           w_out, g_ffn, w_up, conv_w, conv_b, w_down, g_final):
    depth = w_in.shape[0]
    tables = _retention_tables(x.shape[1])
    row = lambda a: a.reshape(1, -1)
    for l in range(depth):
        x, w_up_l, w_down_l = _mixer(
            x, mem, tables, row(g_mix[l]), w_in[l].astype(BF16),
            _pair_block_diag(w_pool[l]).astype(BF16), row(pool_scale[l]), w_a[l].astype(BF16),
            row(g_ret[l]), row(b_ret[l]), w_r[l].astype(BF16), row(g_mem[l]),
            w_mem_kv[l].astype(BF16), w_c[l].astype(BF16), w_out[l].astype(BF16), w_up[l], w_down[l])
        x = _ffn(x, row(g_ffn[l]), w_up_l, conv_w[l], row(conv_b[l]), w_down_l, row(g_final),
                 final_norm=(l == depth - 1))
    return x
```

```python
import functools

import jax
import jax.numpy as jnp
import numpy as np
from jax import lax
from jax.experimental import pallas as pl
from jax.experimental.pallas import tpu as pltpu

D_MODEL = 1024
POOL_WINDOWS = (2, 4, 8, 16)
POOL_GROUP_DIM = D_MODEL // 8
POOL_WIDTH = len(POOL_WINDOWS) * POOL_GROUP_DIM
POOL_HALO = 16
RET_HEADS = 4
RET_QK_DIM = D_MODEL // 8
RET_V_DIM = 2 * RET_QK_DIM
RET_QK_WIDTH = RET_HEADS * RET_QK_DIM
RET_V_WIDTH = RET_HEADS * RET_V_DIM
RET_CHUNK = 128
ROPE_BASE = 10000.0
XA_HEADS = 4
XA_HEAD_DIM = D_MODEL // 8
XA_WIDTH = XA_HEADS * XA_HEAD_DIM
FFN_HIDDEN = 2816
CONV_WIDTH = 3
CONV_HALO = 8
BF16_ROWS = 16
EPS = 1e-6

OFF_HP = 0
OFF_Q = OFF_HP + POOL_WIDTH
OFF_K = OFF_Q + RET_QK_WIDTH
OFF_V = OFF_K + RET_QK_WIDTH
OFF_GR = OFF_V + RET_V_WIDTH
OFF_QX = OFF_GR + RET_V_WIDTH
OFF_GATE = OFF_QX + XA_WIDTH

MIXER_ROWS = 512
FFN_ROWS = 1024
FFN_COLS = 256
OUT_ROWS = 256
V7X_VMEM_LIMIT_BYTES = 56 * 1024 * 1024

BF16 = jnp.bfloat16
F32 = jnp.float32


def _dot(a, b):
    return jnp.dot(a, b, preferred_element_type=F32)


def _rms(x):
    return x * lax.rsqrt(jnp.mean(x * x, axis=-1, keepdims=True) + EPS)


def _sigmoid(x):
    return 0.5 * jnp.tanh(0.5 * x) + 0.5


def _mixer_kernel(x_ref, mem_ref, cos_ref, sin_ref, cost_ref, sint_ref, dintra_ref, qdec_ref, kdect_ref,
                  cdec_ref, g_mix_ref, w_in_ref, w_pool_ref, pool_scale_ref, w_a_ref, g_ret_ref, b_ret_ref,
                  w_r_ref, g_mem_ref, w_mem_kv_ref, w_c_ref, w_out_ref, w_up_ref, w_down_ref,
                  o_ref, w_up_bf_ref, w_down_bf_ref,
                  h_scr, w_kt_scr, e_scr, r_scr, kt_scr, v_scr, kd_scr, gn_scr, m_scr, *, up_chunks, down_chunks):
    ts = x_ref.shape[0]
    s_idx = pl.program_id(1)
    step = pl.program_id(0) * pl.num_programs(1) + s_idx
    pos = pl.ds(pl.multiple_of(s_idx * ts, ts), ts)

    @pl.when(step < up_chunks)
    def _cast_up():
        w_up_bf_ref[...] = w_up_ref[...].astype(BF16)

    @pl.when(step < down_chunks)
    def _cast_down():
        w_down_bf_ref[...] = w_down_ref[...].astype(BF16)

    @pl.when(step == 0)
    def _transpose_key_projection():
        for hh in range(RET_HEADS):
            cols = slice(OFF_K + hh * RET_QK_DIM, OFF_K + (hh + 1) * RET_QK_DIM)
            w_kt_scr[hh * RET_QK_DIM:(hh + 1) * RET_QK_DIM, :] = w_in_ref[:, cols].astype(F32).T.astype(BF16)

    @pl.when(s_idx == 0)
    def _start_of_sequence():
        e_scr[...] = jnp.zeros(e_scr.shape, F32)
        r_scr[...] = jnp.zeros(r_scr.shape, F32)
        mem_n = (_rms(mem_ref[...]) * g_mem_ref[...]).astype(BF16)
        kv = _dot(mem_n, w_mem_kv_ref[...])
        kt_scr[...] = kv[:, :XA_WIDTH].T.astype(BF16)
        v_scr[...] = kv[:, XA_WIDTH:].astype(BF16)

    h_scr[...] = (_rms(x_ref[...]) * g_mix_ref[...]).astype(BF16)

    def proj(off, width):
        return _dot(h_scr[...], w_in_ref[:, off:off + width])

    def rope(a):
        cos = cos_ref[pos, :]
        sin = sin_ref[pos, :]
        parts = []
        for hh in range(RET_HEADS):
            ah = a[:, hh * RET_QK_DIM:(hh + 1) * RET_QK_DIM]
            parts.append(ah * cos + pltpu.roll(ah, RET_QK_DIM // 2, 1) * sin)
        return jnp.concatenate(parts, axis=-1)

    def qslice(hh):
        return slice(hh * RET_QK_DIM, (hh + 1) * RET_QK_DIM)

    def vslice(hh):
        return slice(hh * RET_V_DIM, (hh + 1) * RET_V_DIM)

    def retention_chunk(c, qd, v, gr):
        rows = slice(c * RET_CHUNK, (c + 1) * RET_CHUNK)
        zero = jnp.zeros((RET_QK_DIM, RET_QK_DIM), BF16)
        kdt = [kd_scr[qslice(hh), rows] for hh in range(RET_HEADS)]
        sc = []
        for h0 in range(0, RET_HEADS, 2):
            keys = jnp.concatenate([jnp.concatenate([kdt[h0], zero], axis=1),
                                    jnp.concatenate([zero, kdt[h0 + 1]], axis=1)], axis=0)
            pair = _dot(qd[rows, h0 * RET_QK_DIM:(h0 + 2) * RET_QK_DIM], keys)
            sc += [pair[:, :RET_QK_DIM], pair[:, RET_QK_DIM:]]
        state = [r_scr[:, vslice(hh)] for hh in range(RET_HEADS)]
        upd = [_dot(kdt[hh], v[rows, vslice(hh)]) for hh in range(RET_HEADS)]
        for hh in range(RET_HEADS):
            s_h = (sc[hh] * dintra_ref[hh]).astype(BF16)
            lhs = jnp.concatenate([s_h, qd[rows, qslice(hh)]], axis=1)
            rhs = jnp.concatenate([v[rows, vslice(hh)], state[hh].astype(BF16)], axis=0)
            oh = _dot(lhs, rhs)
            r_scr[:, vslice(hh)] = cdec_ref[:, vslice(hh)] * state[hh] + upd[hh]
            dev = oh - jnp.mean(oh, axis=-1, keepdims=True)
            oh = dev * lax.rsqrt(jnp.mean(dev * dev, axis=-1, keepdims=True) + EPS)
            oh = oh * g_ret_ref[:, vslice(hh)] + b_ret_ref[:, vslice(hh)]
            gh = gr[rows, vslice(hh)]
            gn_scr[rows, vslice(hh)] = (gh * _sigmoid(gh) * oh).astype(BF16)

    n_chunks = ts // RET_CHUNK

    hp = proj(OFF_HP, POOL_WIDTH)
    q = proj(OFF_Q, RET_QK_WIDTH)
    kt = lax.dot_general(w_kt_scr[...], h_scr[...], (((1,), (1,)), ((), ())), preferred_element_type=F32)

    t1 = (lax.broadcasted_iota(jnp.int32, (ts, POOL_GROUP_DIM), 0) + (s_idx * ts + 1)).astype(F32)
    pooled = []
    for gi, w in enumerate(POOL_WINDOWS):
        cols = slice(gi * POOL_GROUP_DIM, (gi + 1) * POOL_GROUP_DIM)
        cur = hp[:, cols]
        acc = jnp.concatenate([e_scr[:, cols], cur], axis=0)
        span = 1
        while span < w:
            acc = acc + pltpu.roll(acc, span, 0)
            span *= 2
        pooled.append(acc[POOL_HALO:, :] / jnp.minimum(t1, float(w)) - cur)
    e_scr[...] = hp[ts - POOL_HALO:, :]
    pooled = jnp.concatenate(pooled, axis=-1).astype(BF16)

    v = proj(OFF_V, RET_V_WIDTH).astype(BF16)

    q = rope(q)
    qd = jnp.concatenate([(q[c * RET_CHUNK:(c + 1) * RET_CHUNK, :] * qdec_ref[...]).astype(BF16)
                          for c in range(n_chunks)], axis=0)
    half = RET_QK_DIM // 2
    for hh in range(RET_HEADS):
        kh = kt[qslice(hh), :]
        swapped = jnp.concatenate([kh[half:, :], kh[:half, :]], axis=0)
        kh = (kh * cost_ref[:, pos] + swapped * sint_ref[:, pos]) * (RET_QK_DIM ** -0.5)
        for c in range(n_chunks):
            cols = slice(c * RET_CHUNK, (c + 1) * RET_CHUNK)
            kd_scr[qslice(hh), cols] = (kh[:, cols] * kdect_ref[qslice(hh), :]).astype(BF16)

    pair_w = 2 * POOL_GROUP_DIM
    y_pool = jnp.concatenate(
        [_dot(pooled[:, i * pair_w:(i + 1) * pair_w], w_pool_ref[i]) for i in range(POOL_WIDTH // pair_w)],
        axis=-1) * pool_scale_ref[...]
    qx = proj(OFF_QX, XA_WIDTH).astype(BF16)
    y_pool = _dot(y_pool.astype(BF16), w_a_ref[...])

    m_scr[...] = _sigmoid(proj(OFF_GATE, D_MODEL)) * y_pool

    probs, denom = [], []
    for hh in range(XA_HEADS):
        hs = qslice(hh)
        sc = _dot(qx[:, hs], kt_scr[hs, :]) * (XA_HEAD_DIM ** -0.5)
        p = jnp.exp(sc - jnp.max(sc, axis=-1, keepdims=True))
        probs.append(p.astype(BF16))
        denom.append(jnp.sum(p, axis=-1, keepdims=True))

    gr = proj(OFF_GR, RET_V_WIDTH)

    mem_parts = {}

    def attend():
        heads = [_dot(probs[hh], v_scr[:, qslice(hh)]) / denom[hh] for hh in range(XA_HEADS)]
        mem_parts["heads"] = jnp.concatenate(heads, axis=-1).astype(BF16)

    def gate_mem():
        mem_parts["gate"] = _sigmoid(proj(OFF_GATE + 2 * D_MODEL, D_MODEL))

    def add_mem():
        m_scr[...] += mem_parts["gate"] * _dot(mem_parts["heads"], w_c_ref[...])

    fills = [attend, gate_mem, add_mem]
    for c in range(n_chunks):
        retention_chunk(c, qd, v, gr)
        for fill in (fills[c:c + 1] if c < n_chunks - 1 else fills[c:]):
            fill()

    gate_ret = _sigmoid(proj(OFF_GATE + D_MODEL, D_MODEL))
    y_ret = _dot(gn_scr[...], w_r_ref[...])
    merged = (m_scr[...] + gate_ret * y_ret).astype(BF16)

    for r0 in range(0, ts, OUT_ROWS):
        rows = slice(r0, r0 + OUT_ROWS)
        o_ref[rows, :] = x_ref[rows, :] + _dot(merged[rows, :], w_out_ref[...])


def _retention_tables(seq):
    half = RET_QK_DIM // 2
    pos = np.arange(seq, dtype=np.float64)
    inv = ROPE_BASE ** (-np.arange(half, dtype=np.float64) / half)
    ang = pos[:, None] * inv[None, :]
    cos, sin = np.cos(ang), np.sin(ang)
    cos2 = np.concatenate([cos, cos], axis=-1)
    sin2 = np.concatenate([-sin, sin], axis=-1)
    c = RET_CHUNK
    log_gamma = np.log1p(-np.exp2(-5.0 - np.arange(RET_HEADS, dtype=np.float64)))
    lg = log_gamma[:, None, None]
    idx = np.arange(c, dtype=np.float64)
    rel = idx[:, None] - idx[None, :]
    q_decay = np.exp((idx + 1.0)[None, :, None] * lg)
    k_decay = np.exp((c - 1.0 - idx)[None, :, None] * lg)
    chunk_decay = np.exp(c * lg)
    decay_intra = np.where(rel >= 0, 1.0 / chunk_decay, 0.0)

    def lanes(a, width):
        r = a.shape[1]
        return np.broadcast_to(a.transpose(1, 0, 2), (r, RET_HEADS, width)).reshape(r, RET_HEADS * width)

    k_decay_t = np.broadcast_to(k_decay.transpose(0, 2, 1), (RET_HEADS, RET_QK_DIM, c))
    tables = (cos2, sin2, cos2.T, sin2.T, decay_intra, lanes(q_decay, RET_QK_DIM),
              k_decay_t.reshape(RET_QK_WIDTH, c), lanes(chunk_decay, RET_V_DIM))
    return tuple(jnp.asarray(np.ascontiguousarray(t), F32) for t in tables)


def _vmem_full():
    return pl.BlockSpec(memory_space=pltpu.VMEM)


def _cast_chunks(rows, n_steps):
    tiles = rows // BF16_ROWS
    assert rows % BF16_ROWS == 0
    return max(c for c in range(1, min(tiles, n_steps) + 1) if tiles % c == 0)


def _mixer(x, mem, tables, g_mix, w_in, w_pool_bd, pool_scale, w_a, g_ret, b_ret, w_r, g_mem,
           w_mem_kv, w_c, w_out, w_up, w_down):
    b, s, d = x.shape
    m = mem.shape[1]
    ts = min(MIXER_ROWS, s)
    assert s % ts == 0 and ts % RET_CHUNK == 0 and ts % OUT_ROWS == 0
    ns = s // ts
    up_chunks = _cast_chunks(w_up.shape[0], b * ns)
    down_chunks = _cast_chunks(w_down.shape[0], b * ns)

    def cast_spec(w, chunks):
        return pl.BlockSpec((w.shape[0] // chunks, w.shape[1]),
                            lambda bi, si: (jnp.minimum(bi * ns + si, chunks - 1), 0))

    cos2, sin2, cos2t, sin2t, dintra, qdec, kdect, cdec = tables
    row_tile = pl.BlockSpec((None, ts, d), lambda bi, si: (bi, si, 0))
    in_specs = [row_tile,
                pl.BlockSpec((None, m, d), lambda bi, si: (bi, 0, 0)),
                ] + [_vmem_full()] * 20 + [
                cast_spec(w_up, up_chunks), cast_spec(w_down, down_chunks)]
    return pl.pallas_call(
        functools.partial(_mixer_kernel, up_chunks=up_chunks, down_chunks=down_chunks),
        grid=(b, ns),
        in_specs=in_specs,
        out_specs=[row_tile, cast_spec(w_up, up_chunks), cast_spec(w_down, down_chunks)],
        out_shape=[jax.ShapeDtypeStruct((b, s, d), F32),
                   jax.ShapeDtypeStruct(w_up.shape, BF16),
                   jax.ShapeDtypeStruct(w_down.shape, BF16)],
        scratch_shapes=[
            pltpu.VMEM((ts, d), BF16),
            pltpu.VMEM((RET_QK_WIDTH, d), BF16),
            pltpu.VMEM((POOL_HALO, POOL_WIDTH), F32),
            pltpu.VMEM((RET_QK_DIM, RET_V_WIDTH), F32),
            pltpu.VMEM((XA_WIDTH, m), BF16),
            pltpu.VMEM((m, XA_WIDTH), BF16),
            pltpu.VMEM((RET_QK_WIDTH, ts), BF16),
            pltpu.VMEM((ts, RET_V_WIDTH), BF16),
            pltpu.VMEM((ts, d), F32),
        ],
        compiler_params=pltpu.CompilerParams(
            dimension_semantics=("arbitrary", "arbitrary"),
            vmem_limit_bytes=V7X_VMEM_LIMIT_BYTES),
        name="mixer",
    )(x, mem, cos2, sin2, cos2t, sin2t, dintra, qdec, kdect, cdec,
      g_mix, w_in, w_pool_bd, pool_scale, w_a, g_ret, b_ret, w_r, g_mem, w_mem_kv, w_c, w_out,
      w_up, w_down)


def _ffn_kernel(x_ref, g_ffn_ref, w_up_ref, conv_w_ref, conv_b_ref, w_down_ref, g_out_ref,
                o_ref, h_scr, a_scr, carry_scr, g_scr, *, final_norm):
    ts = x_ref.shape[0]

    @pl.when(pl.program_id(1) == 0)
    def _start_of_sequence():
        carry_scr[...] = jnp.zeros(carry_scr.shape, F32)

    h_scr[...] = (_rms(x_ref[...]) * g_ffn_ref[...]).astype(BF16)
    for j in range(FFN_HIDDEN // FFN_COLS):
        cs = slice(j * FFN_COLS, (j + 1) * FFN_COLS)
        a = _dot(h_scr[...], w_up_ref[:, cs])
        gate = _dot(h_scr[...], w_up_ref[:, FFN_HIDDEN + j * FFN_COLS:FFN_HIDDEN + (j + 1) * FFN_COLS])
        buf = a_scr.at[j % 2]
        buf[0:CONV_HALO, :] = carry_scr[:, cs]
        buf[CONV_HALO:CONV_HALO + ts, :] = a
        carry_scr[:, cs] = a[ts - CONV_HALO:, :]
        conv = a * conv_w_ref[CONV_WIDTH - 1:CONV_WIDTH, cs] + conv_b_ref[:, cs]
        for tap in range(CONV_WIDTH - 1):
            back = CONV_WIDTH - 1 - tap
            conv = conv + buf[CONV_HALO - back:CONV_HALO - back + ts, :] * conv_w_ref[tap:tap + 1, cs]
        g_scr[:, cs] = (jax.nn.gelu(conv) * gate).astype(BF16)
    for r0 in range(0, ts, OUT_ROWS):
        rows = slice(r0, r0 + OUT_ROWS)
        y = x_ref[rows, :] + _dot(g_scr[rows, :], w_down_ref[...])
        if final_norm:
            y = _rms(y) * g_out_ref[...]
        o_ref[rows, :] = y


def _ffn(x, g_ffn, w_up, conv_w, conv_b, w_down, g_out, final_norm):
    b, s, d = x.shape
    ts = min(FFN_ROWS, s)
    assert s % ts == 0 and FFN_HIDDEN % FFN_COLS == 0 and ts % OUT_ROWS == 0
    row_tile = pl.BlockSpec((None, ts, d), lambda bi, si: (bi, si, 0))
    return pl.pallas_call(
        functools.partial(_ffn_kernel, final_norm=final_norm),
        grid=(b, s // ts),
        in_specs=[row_tile] + [_vmem_full()] * 6,
        out_specs=row_tile,
        out_shape=jax.ShapeDtypeStruct((b, s, d), F32),
        scratch_shapes=[
            pltpu.VMEM((ts, d), BF16),
            pltpu.VMEM((2, ts + CONV_HALO, FFN_COLS), F32),
            pltpu.VMEM((CONV_HALO, FFN_HIDDEN), F32),
            pltpu.VMEM((ts, FFN_HIDDEN), BF16),
        ],
        compiler_params=pltpu.CompilerParams(
            dimension_semantics=("arbitrary", "arbitrary"),
            vmem_limit_bytes=V7X_VMEM_LIMIT_BYTES),
        name="ffn",
    )(x, g_ffn, w_up, conv_w, conv_b, w_down, g_out)


def _pair_block_diag(w):
    g, c, dd = w.shape
    zero = jnp.zeros((g // 2, c, dd), w.dtype)
    top = jnp.concatenate([w[0::2], zero], axis=2)
    bottom = jnp.concatenate([zero, w[1::2]], axis=2)
    return jnp.concatenate([top, bottom], axis=1)


def kernel(x, mem, g_mix, w_in, w_pool, pool_scale, w_a, g_ret, b_ret, w_r, g_mem, w_mem_kv, w_c,
           w_out, g_ffn, w_up, conv_w, conv_b, w_down, g_final):
    depth = w_in.shape[0]
    tables = _retention_tables(x.shape[1])
    row = lambda a: a.reshape(1, -1)
    for l in range(depth):
        x, w_up_l, w_down_l = _mixer(
            x, mem, tables, row(g_mix[l]), w_in[l].astype(BF16),
            _pair_block_diag(w_pool[l]).astype(BF16), row(pool_scale[l]), w_a[l].astype(BF16),
            row(g_ret[l]), row(b_ret[l]), w_r[l].astype(BF16), row(g_mem[l]),
            w_mem_kv[l].astype(BF16), w_c[l].astype(BF16), w_out[l].astype(BF16), w_up[l], w_down[l])
        x = _ffn(x, row(g_ffn[l]), w_up_l, conv_w[l], row(conv_b[l]), w_down_l, row(g_final),
                 final_norm=(l == depth - 1))
    return x
```

```python
import functools

import jax
import jax.numpy as jnp
import numpy as np
from jax import lax
from jax.experimental import pallas as pl
from jax.experimental.pallas import tpu as pltpu

D_MODEL = 1024
POOL_WINDOWS = (2, 4, 8, 16)
POOL_GROUP_DIM = D_MODEL // 8
POOL_WIDTH = len(POOL_WINDOWS) * POOL_GROUP_DIM
POOL_HALO = 16
RET_HEADS = 4
RET_QK_DIM = D_MODEL // 8
RET_V_DIM = 2 * RET_QK_DIM
RET_QK_WIDTH = RET_HEADS * RET_QK_DIM
RET_V_WIDTH = RET_HEADS * RET_V_DIM
RET_CHUNK = 128
ROPE_BASE = 10000.0
XA_HEADS = 4
XA_HEAD_DIM = D_MODEL // 8
XA_WIDTH = XA_HEADS * XA_HEAD_DIM
FFN_HIDDEN = 2816
CONV_WIDTH = 3
CONV_HALO = 8
BF16_ROWS = 16
EPS = 1e-6

OFF_HP = 0
OFF_Q = OFF_HP + POOL_WIDTH
OFF_K = OFF_Q + RET_QK_WIDTH
OFF_V = OFF_K + RET_QK_WIDTH
OFF_GR = OFF_V + RET_V_WIDTH
OFF_QX = OFF_GR + RET_V_WIDTH
OFF_GATE = OFF_QX + XA_WIDTH

MIXER_ROWS = 512
FFN_ROWS = 1024
FFN_COLS = 256
OUT_ROWS = 256
V7X_VMEM_LIMIT_BYTES = 56 * 1024 * 1024

BF16 = jnp.bfloat16
F32 = jnp.float32


def _dot(a, b):
    return jnp.dot(a, b, preferred_element_type=F32)


def _rms(x):
    return x * lax.rsqrt(jnp.mean(x * x, axis=-1, keepdims=True) + EPS)


def _sigmoid(x):
    return 0.5 * jnp.tanh(0.5 * x) + 0.5


def _mixer_kernel(x_ref, mem_ref, cos_ref, sin_ref, cost_ref, sint_ref, dintra_ref, qdec_ref, kdect_ref,
                  cdec_ref, g_mix_ref, w_in_ref, w_pool_ref, pool_scale_ref, w_a_ref, g_ret_ref, b_ret_ref,
                  w_r_ref, g_mem_ref, w_mem_kv_ref, w_c_ref, w_out_ref, w_up_ref, w_down_ref,
                  o_ref, w_up_bf_ref, w_down_bf_ref,
                  h_scr, w_kt_scr, w_qxt_scr, e_scr, r_scr, k_scr, vt_scr, kd_scr, gn_scr, m_scr, *, up_chunks, down_chunks):
    ts = x_ref.shape[0]
    s_idx = pl.program_id(1)
    step = pl.program_id(0) * pl.num_programs(1) + s_idx

    @pl.when(step < up_chunks)
    def _cast_up():
        w_up_bf_ref[...] = w_up_ref[...].astype(BF16)

    @pl.when(step < down_chunks)
    def _cast_down():
        w_down_bf_ref[...] = w_down_ref[...].astype(BF16)

    @pl.when(step == 0)
    def _transpose_key_projection():
        for hh in range(RET_HEADS):
            cols = slice(OFF_K + hh * RET_QK_DIM, OFF_K + (hh + 1) * RET_QK_DIM)
            w_kt_scr[hh * RET_QK_DIM:(hh + 1) * RET_QK_DIM, :] = w_in_ref[:, cols].astype(F32).T.astype(BF16)
        for hh in range(XA_HEADS):
            cols = slice(OFF_QX + hh * XA_HEAD_DIM, OFF_QX + (hh + 1) * XA_HEAD_DIM)
            w_qxt_scr[hh * XA_HEAD_DIM:(hh + 1) * XA_HEAD_DIM, :] = w_in_ref[:, cols].astype(F32).T.astype(BF16)

    @pl.when(s_idx == 0)
    def _start_of_sequence():
        e_scr[...] = jnp.zeros(e_scr.shape, F32)
        r_scr[...] = jnp.zeros(r_scr.shape, F32)
        mem_n = (_rms(mem_ref[...]) * g_mem_ref[...]).astype(BF16)
        kv = _dot(mem_n, w_mem_kv_ref[...])
        k_scr[...] = kv[:, :XA_WIDTH].astype(BF16)
        vt_scr[...] = kv[:, XA_WIDTH:].T.astype(BF16)

    h_scr[...] = (_rms(x_ref[...]) * g_mix_ref[...]).astype(BF16)

    def proj(off, width):
        return _dot(h_scr[...], w_in_ref[:, off:off + width])

    def rope(a):
        cos = cos_ref[...]
        sin = sin_ref[...]
        parts = []
        for hh in range(RET_HEADS):
            ah = a[:, hh * RET_QK_DIM:(hh + 1) * RET_QK_DIM]
            parts.append(ah * cos + pltpu.roll(ah, RET_QK_DIM // 2, 1) * sin)
        return jnp.concatenate(parts, axis=-1)

    def qslice(hh):
        return slice(hh * RET_QK_DIM, (hh + 1) * RET_QK_DIM)

    def vslice(hh):
        return slice(hh * RET_V_DIM, (hh + 1) * RET_V_DIM)

    def retention_chunk(c, qd, v, gr):
        rows = slice(c * RET_CHUNK, (c + 1) * RET_CHUNK)
        zero = jnp.zeros((RET_QK_DIM, RET_QK_DIM), BF16)
        kdt = [kd_scr[qslice(hh), rows] for hh in range(RET_HEADS)]
        sc = []
        for h0 in range(0, RET_HEADS, 2):
            keys = jnp.concatenate([jnp.concatenate([kdt[h0], zero], axis=1),
                                    jnp.concatenate([zero, kdt[h0 + 1]], axis=1)], axis=0)
            pair = _dot(qd[rows, h0 * RET_QK_DIM:(h0 + 2) * RET_QK_DIM], keys)
            sc += [pair[:, :RET_QK_DIM], pair[:, RET_QK_DIM:]]
        state = [r_scr[:, vslice(hh)] for hh in range(RET_HEADS)]
        upd = [_dot(kdt[hh], v[rows, vslice(hh)]) for hh in range(RET_HEADS)]
        for hh in range(RET_HEADS):
            s_h = (sc[hh] * dintra_ref[hh]).astype(BF16)
            lhs = jnp.concatenate([s_h, qd[rows, qslice(hh)]], axis=1)
            rhs = jnp.concatenate([v[rows, vslice(hh)], state[hh].astype(BF16)], axis=0)
            oh = _dot(lhs, rhs)
            r_scr[:, vslice(hh)] = cdec_ref[:, vslice(hh)] * state[hh] + upd[hh]
            dev = oh - jnp.mean(oh, axis=-1, keepdims=True)
            oh = dev * lax.rsqrt(jnp.mean(dev * dev, axis=-1, keepdims=True) + EPS)
            oh = oh * g_ret_ref[:, vslice(hh)] + b_ret_ref[:, vslice(hh)]
            gh = gr[rows, vslice(hh)]
            gn_scr[rows, vslice(hh)] = (gh * _sigmoid(gh) * oh).astype(BF16)

    n_chunks = ts // RET_CHUNK

    hp = proj(OFF_HP, POOL_WIDTH)
    q = proj(OFF_Q, RET_QK_WIDTH)
    kt = lax.dot_general(w_kt_scr[...], h_scr[...], (((1,), (1,)), ((), ())), preferred_element_type=F32)

    t1 = (lax.broadcasted_iota(jnp.int32, (ts, POOL_GROUP_DIM), 0) + (s_idx * ts + 1)).astype(F32)
    pooled = []
    for gi, w in enumerate(POOL_WINDOWS):
        cols = slice(gi * POOL_GROUP_DIM, (gi + 1) * POOL_GROUP_DIM)
        cur = hp[:, cols]
        acc = jnp.concatenate([e_scr[:, cols], cur], axis=0)
        span = 1
        while span < w:
            acc = acc + pltpu.roll(acc, span, 0)
            span *= 2
        pooled.append(acc[POOL_HALO:, :] / jnp.minimum(t1, float(w)) - cur)
    e_scr[...] = hp[ts - POOL_HALO:, :]
    pooled = jnp.concatenate(pooled, axis=-1).astype(BF16)

    v = proj(OFF_V, RET_V_WIDTH).astype(BF16)

    q = rope(q)
    qd = jnp.concatenate([(q[c * RET_CHUNK:(c + 1) * RET_CHUNK, :] * qdec_ref[...]).astype(BF16)
                          for c in range(n_chunks)], axis=0)
    half = RET_QK_DIM // 2
    for hh in range(RET_HEADS):
        kh = kt[qslice(hh), :]
        swapped = jnp.concatenate([kh[half:, :], kh[:half, :]], axis=0)
        kh = (kh * cost_ref[...] + swapped * sint_ref[...]) * (RET_QK_DIM ** -0.5)
        for c in range(n_chunks):
            cols = slice(c * RET_CHUNK, (c + 1) * RET_CHUNK)
            kd_scr[qslice(hh), cols] = (kh[:, cols] * kdect_ref[qslice(hh), :]).astype(BF16)

    pair_w = 2 * POOL_GROUP_DIM
    y_pool = jnp.concatenate(
        [_dot(pooled[:, i * pair_w:(i + 1) * pair_w], w_pool_ref[i]) for i in range(POOL_WIDTH // pair_w)],
        axis=-1) * pool_scale_ref[...]
    qxt = lax.dot_general(w_qxt_scr[...], h_scr[...], (((1,), (1,)), ((), ())),
                          preferred_element_type=F32).astype(BF16)
    y_pool = _dot(y_pool.astype(BF16), w_a_ref[...])

    m_scr[...] = _sigmoid(proj(OFF_GATE, D_MODEL)) * y_pool

    probs, denom = [], []
    for hh in range(XA_HEADS):
        hs = qslice(hh)
        sc = _dot(k_scr[:, hs], qxt[hs, :]) * (XA_HEAD_DIM ** -0.5)
        p = jnp.exp(sc - jnp.max(sc, axis=0, keepdims=True))
        probs.append(p.astype(BF16))
        denom.append(jnp.sum(p, axis=0, keepdims=True))

    gr = proj(OFF_GR, RET_V_WIDTH)

    mem_parts = {}

    def attend():
        heads = [_dot(vt_scr[qslice(hh), :], probs[hh]) / denom[hh] for hh in range(XA_HEADS)]
        mem_parts["heads"] = jnp.concatenate(heads, axis=0).astype(BF16)

    def gate_mem():
        mem_parts["gate"] = _sigmoid(proj(OFF_GATE + 2 * D_MODEL, D_MODEL))

    def add_mem():
        y_mem = lax.dot_general(mem_parts["heads"], w_c_ref[...], (((0,), (0,)), ((), ())),
                                preferred_element_type=F32)
        m_scr[...] += mem_parts["gate"] * y_mem

    fills = [attend, gate_mem, add_mem]
    for c in range(n_chunks):
        retention_chunk(c, qd, v, gr)
        for fill in (fills[c:c + 1] if c < n_chunks - 1 else fills[c:]):
            fill()

    gate_ret = _sigmoid(proj(OFF_GATE + D_MODEL, D_MODEL))
    y_ret = _dot(gn_scr[...], w_r_ref[...])
    merged = (m_scr[...] + gate_ret * y_ret).astype(BF16)

    for r0 in range(0, ts, OUT_ROWS):
        rows = slice(r0, r0 + OUT_ROWS)
        o_ref[rows, :] = x_ref[rows, :] + _dot(merged[rows, :], w_out_ref[...])


def _retention_tables(seq):
    half = RET_QK_DIM // 2
    pos = np.arange(seq, dtype=np.float64)
    inv = ROPE_BASE ** (-np.arange(half, dtype=np.float64) / half)
    ang = pos[:, None] * inv[None, :]
    cos, sin = np.cos(ang), np.sin(ang)
    cos2 = np.concatenate([cos, cos], axis=-1)
    sin2 = np.concatenate([-sin, sin], axis=-1)
    c = RET_CHUNK
    log_gamma = np.log1p(-np.exp2(-5.0 - np.arange(RET_HEADS, dtype=np.float64)))
    lg = log_gamma[:, None, None]
    idx = np.arange(c, dtype=np.float64)
    rel = idx[:, None] - idx[None, :]
    q_decay = np.exp((idx + 1.0)[None, :, None] * lg)
    k_decay = np.exp((c - 1.0 - idx)[None, :, None] * lg)
    chunk_decay = np.exp(c * lg)
    decay_intra = np.where(rel >= 0, 1.0 / chunk_decay, 0.0)

    def lanes(a, width):
        r = a.shape[1]
        return np.broadcast_to(a.transpose(1, 0, 2), (r, RET_HEADS, width)).reshape(r, RET_HEADS * width)

    k_decay_t = np.broadcast_to(k_decay.transpose(0, 2, 1), (RET_HEADS, RET_QK_DIM, c))
    tables = (cos2, sin2, cos2.T, sin2.T, decay_intra, lanes(q_decay, RET_QK_DIM),
              k_decay_t.reshape(RET_QK_WIDTH, c), lanes(chunk_decay, RET_V_DIM))
    return tuple(jnp.asarray(np.ascontiguousarray(t), F32) for t in tables)


def _vmem_full():
    return pl.BlockSpec(memory_space=pltpu.VMEM)


def _cast_chunks(rows, n_steps):
    tiles = rows // BF16_ROWS
    assert rows % BF16_ROWS == 0
    return max(c for c in range(1, min(tiles, n_steps) + 1) if tiles % c == 0)


def _mixer(x, mem, tables, g_mix, w_in, w_pool_bd, pool_scale, w_a, g_ret, b_ret, w_r, g_mem,
           w_mem_kv, w_c, w_out, w_up, w_down):
    b, s, d = x.shape
    m = mem.shape[1]
    ts = min(MIXER_ROWS, s)
    assert s % ts == 0 and ts % RET_CHUNK == 0 and ts % OUT_ROWS == 0
    ns = s // ts
    up_chunks = _cast_chunks(w_up.shape[0], b * ns)
    down_chunks = _cast_chunks(w_down.shape[0], b * ns)

    def cast_spec(w, chunks):
        return pl.BlockSpec((w.shape[0] // chunks, w.shape[1]),
                            lambda bi, si: (jnp.minimum(bi * ns + si, chunks - 1), 0))

    cos2, sin2, cos2t, sin2t, dintra, qdec, kdect, cdec = tables
    row_tile = pl.BlockSpec((None, ts, d), lambda bi, si: (bi, si, 0))
    pos_tile = pl.BlockSpec((ts, RET_QK_DIM), lambda bi, si: (si, 0))
    pos_tile_t = pl.BlockSpec((RET_QK_DIM, ts), lambda bi, si: (0, si))
    in_specs = [row_tile,
                pl.BlockSpec((None, m, d), lambda bi, si: (bi, 0, 0)),
                pos_tile, pos_tile, pos_tile_t, pos_tile_t] + [_vmem_full()] * 16 + [
                cast_spec(w_up, up_chunks), cast_spec(w_down, down_chunks)]
    return pl.pallas_call(
        functools.partial(_mixer_kernel, up_chunks=up_chunks, down_chunks=down_chunks),
        grid=(b, ns),
        in_specs=in_specs,
        out_specs=[row_tile, cast_spec(w_up, up_chunks), cast_spec(w_down, down_chunks)],
        out_shape=[jax.ShapeDtypeStruct((b, s, d), F32),
                   jax.ShapeDtypeStruct(w_up.shape, BF16),
                   jax.ShapeDtypeStruct(w_down.shape, BF16)],
        scratch_shapes=[
            pltpu.VMEM((ts, d), BF16),
            pltpu.VMEM((RET_QK_WIDTH, d), BF16),
            pltpu.VMEM((XA_WIDTH, d), BF16),
            pltpu.VMEM((POOL_HALO, POOL_WIDTH), F32),
            pltpu.VMEM((RET_QK_DIM, RET_V_WIDTH), F32),
            pltpu.VMEM((m, XA_WIDTH), BF16),
            pltpu.VMEM((XA_WIDTH, m), BF16),
            pltpu.VMEM((RET_QK_WIDTH, ts), BF16),
            pltpu.VMEM((ts, RET_V_WIDTH), BF16),
            pltpu.VMEM((ts, d), F32),
        ],
        compiler_params=pltpu.CompilerParams(
            dimension_semantics=("arbitrary", "arbitrary"),
            vmem_limit_bytes=V7X_VMEM_LIMIT_BYTES),
        name="mixer",
    )(x, mem, cos2, sin2, cos2t, sin2t, dintra, qdec, kdect, cdec,
      g_mix, w_in, w_pool_bd, pool_scale, w_a, g_ret, b_ret, w_r, g_mem, w_mem_kv, w_c, w_out,
      w_up, w_down)


def _ffn_kernel(x_ref, g_ffn_ref, w_up_ref, conv_w_ref, conv_b_ref, w_down_ref, g_out_ref,
                o_ref, h_scr, a_scr, carry_scr, g_scr, *, final_norm):
    ts = x_ref.shape[0]

    @pl.when(pl.program_id(1) == 0)
    def _start_of_sequence():
        carry_scr[...] = jnp.zeros(carry_scr.shape, F32)

    h_scr[...] = (_rms(x_ref[...]) * g_ffn_ref[...]).astype(BF16)
    for j in range(FFN_HIDDEN // FFN_COLS):
        cs = slice(j * FFN_COLS, (j + 1) * FFN_COLS)
        a = _dot(h_scr[...], w_up_ref[:, cs])
        gate = _dot(h_scr[...], w_up_ref[:, FFN_HIDDEN + j * FFN_COLS:FFN_HIDDEN + (j + 1) * FFN_COLS])
        buf = a_scr.at[j % 2]
        buf[0:CONV_HALO, :] = carry_scr[:, cs]
        buf[CONV_HALO:CONV_HALO + ts, :] = a
        carry_scr[:, cs] = a[ts - CONV_HALO:, :]
        conv = a * conv_w_ref[CONV_WIDTH - 1:CONV_WIDTH, cs] + conv_b_ref[:, cs]
        for tap in range(CONV_WIDTH - 1):
            back = CONV_WIDTH - 1 - tap
            conv = conv + buf[CONV_HALO - back:CONV_HALO - back + ts, :] * conv_w_ref[tap:tap + 1, cs]
        g_scr[:, cs] = (jax.nn.gelu(conv) * gate).astype(BF16)
    for r0 in range(0, ts, OUT_ROWS):
        rows = slice(r0, r0 + OUT_ROWS)
        y = x_ref[rows, :] + _dot(g_scr[rows, :], w_down_ref[...])
        if final_norm:
            y = _rms(y) * g_out_ref[...]
        o_ref[rows, :] = y


def _ffn(x, g_ffn, w_up, conv_w, conv_b, w_down, g_out, final_norm):
    b, s, d = x.shape
    ts = min(FFN_ROWS, s)
    assert s % ts == 0 and FFN_HIDDEN % FFN_COLS == 0 and ts % OUT_ROWS == 0
    row_tile = pl.BlockSpec((None, ts, d), lambda bi, si: (bi, si, 0))
    return pl.pallas_call(
        functools.partial(_ffn_kernel, final_norm=final_norm),
        grid=(b, s // ts),
        in_specs=[row_tile] + [_vmem_full()] * 6,
        out_specs=row_tile,
        out_shape=jax.ShapeDtypeStruct((b, s, d), F32),
        scratch_shapes=[
            pltpu.VMEM((ts, d), BF16),
            pltpu.VMEM((2, ts + CONV_HALO, FFN_COLS), F32),
            pltpu.VMEM((CONV_HALO, FFN_HIDDEN), F32),
            pltpu.VMEM((ts, FFN_HIDDEN), BF16),
        ],
        compiler_params=pltpu.CompilerParams(
            dimension_semantics=("arbitrary", "arbitrary"),
            vmem_limit_bytes=V7X_VMEM_LIMIT_BYTES),
        name="ffn",
    )(x, g_ffn, w_up, conv_w, conv_b, w_down, g_out)


def _pair_block_diag(w):
    g, c, dd = w.shape
    zero = jnp.zeros((g // 2, c, dd), w.dtype)
    top = jnp.concatenate([w[0::2], zero], axis=2)
    bottom = jnp.concatenate([zero, w[1::2]], axis=2)
    return jnp.concatenate([top, bottom], axis=1)


def kernel(x, mem, g_mix, w_in, w_pool, pool_scale, w_a, g_ret, b_ret, w_r, g_mem, w_mem_kv, w_c,
           w_out, g_ffn, w_up, conv_w, conv_b, w_down, g_final):
    depth = w_in.shape[0]
    tables = _retention_tables(x.shape[1])
    row = lambda a: a.reshape(1, -1)
    for l in range(depth):
        x, w_up_l, w_down_l = _mixer(
            x, mem, tables, row(g_mix[l]), w_in[l].astype(BF16),
            _pair_block_diag(w_pool[l]).astype(BF16), row(pool_scale[l]), w_a[l].astype(BF16),
            row(g_ret[l]), row(b_ret[l]), w_r[l].astype(BF16), row(g_mem[l]),
            w_mem_kv[l].astype(BF16), w_c[l].astype(BF16), w_out[l].astype(BF16), w_up[l], w_down[l])
        x = _ffn(x, row(g_ffn[l]), w_up_l, conv_w[l], row(conv_b[l]), w_down_l, row(g_final),
                 final_norm=(l == depth - 1))
    return x
```

```python
import functools

import jax
import jax.numpy as jnp
import numpy as np
from jax import lax
from jax.experimental import pallas as pl
from jax.experimental.pallas import tpu as pltpu

D_MODEL = 1024
POOL_WINDOWS = (2, 4, 8, 16)
POOL_GROUP_DIM = D_MODEL // 8
POOL_WIDTH = len(POOL_WINDOWS) * POOL_GROUP_DIM
POOL_HALO = 16
RET_HEADS = 4
RET_QK_DIM = D_MODEL // 8
RET_V_DIM = 2 * RET_QK_DIM
RET_QK_WIDTH = RET_HEADS * RET_QK_DIM
RET_V_WIDTH = RET_HEADS * RET_V_DIM
RET_CHUNK = 128
ROPE_BASE = 10000.0
XA_HEADS = 4
XA_HEAD_DIM = D_MODEL // 8
XA_WIDTH = XA_HEADS * XA_HEAD_DIM
FFN_HIDDEN = 2816
CONV_WIDTH = 3
CONV_HALO = 8
BF16_ROWS = 16
EPS = 1e-6

OFF_HP = 0
OFF_Q = OFF_HP + POOL_WIDTH
OFF_K = OFF_Q + RET_QK_WIDTH
OFF_V = OFF_K + RET_QK_WIDTH
OFF_GR = OFF_V + RET_V_WIDTH
OFF_QX = OFF_GR + RET_V_WIDTH
OFF_GATE = OFF_QX + XA_WIDTH

MIXER_ROWS = 512
FFN_ROWS = 1024
FFN_COLS = 256
OUT_ROWS = 256
V7X_VMEM_LIMIT_BYTES = 56 * 1024 * 1024

BF16 = jnp.bfloat16
F32 = jnp.float32


def _dot(a, b):
    return jnp.dot(a, b, preferred_element_type=F32)


def _rms(x):
    return x * lax.rsqrt(jnp.mean(x * x, axis=-1, keepdims=True) + EPS)


def _sigmoid(x):
    return 0.5 * jnp.tanh(0.5 * x) + 0.5


def _mixer_kernel(x_ref, mem_ref, cos_ref, sin_ref, cost_ref, sint_ref, dintra_ref, qdec_ref, kdect_ref,
                  cdec_ref, g_mix_ref, w_in_ref, w_pool_ref, pool_scale_ref, w_a_ref, g_ret_ref, b_ret_ref,
                  w_r_ref, g_mem_ref, w_mem_kv_ref, w_c_ref, w_out_ref, w_up_ref, w_down_ref,
                  o_ref, w_up_bf_ref, w_down_bf_ref,
                  h_scr, e_scr, r_scr, kt_scr, v_scr, kd_scr, gn_scr, m_scr, *, up_chunks, down_chunks):
    ts = x_ref.shape[0]
    s_idx = pl.program_id(1)
    step = pl.program_id(0) * pl.num_programs(1) + s_idx

    @pl.when(step < up_chunks)
    def _cast_up():
        w_up_bf_ref[...] = w_up_ref[...].astype(BF16)

    @pl.when(step < down_chunks)
    def _cast_down():
        w_down_bf_ref[...] = w_down_ref[...].astype(BF16)

    @pl.when(s_idx == 0)
    def _start_of_sequence():
        e_scr[...] = jnp.zeros(e_scr.shape, F32)
        r_scr[...] = jnp.zeros(r_scr.shape, F32)
        mem_n = (_rms(mem_ref[...]) * g_mem_ref[...]).astype(BF16)
        kv = _dot(mem_n, w_mem_kv_ref[...])
        kt_scr[...] = kv[:, :XA_WIDTH].T.astype(BF16)
        v_scr[...] = kv[:, XA_WIDTH:].astype(BF16)

    h_scr[...] = (_rms(x_ref[...]) * g_mix_ref[...]).astype(BF16)

    def proj(off, width):
        return _dot(h_scr[...], w_in_ref[:, off:off + width])

    def rope(a):
        cos = cos_ref[...]
        sin = sin_ref[...]
        parts = []
        for hh in range(RET_HEADS):
            ah = a[:, hh * RET_QK_DIM:(hh + 1) * RET_QK_DIM]
            parts.append(ah * cos + pltpu.roll(ah, RET_QK_DIM // 2, 1) * sin)
        return jnp.concatenate(parts, axis=-1)

    def qslice(hh):
        return slice(hh * RET_QK_DIM, (hh + 1) * RET_QK_DIM)

    def vslice(hh):
        return slice(hh * RET_V_DIM, (hh + 1) * RET_V_DIM)

    def retention_chunk(c, qd, v, gr):
        rows = slice(c * RET_CHUNK, (c + 1) * RET_CHUNK)
        zero = jnp.zeros((RET_QK_DIM, RET_QK_DIM), BF16)
        kdt = [kd_scr[qslice(hh), rows] for hh in range(RET_HEADS)]
        sc = []
        for h0 in range(0, RET_HEADS, 2):
            keys = jnp.concatenate([jnp.concatenate([kdt[h0], zero], axis=1),
                                    jnp.concatenate([zero, kdt[h0 + 1]], axis=1)], axis=0)
            pair = _dot(qd[rows, h0 * RET_QK_DIM:(h0 + 2) * RET_QK_DIM], keys)
            sc += [pair[:, :RET_QK_DIM], pair[:, RET_QK_DIM:]]
        state = [r_scr[:, vslice(hh)] for hh in range(RET_HEADS)]
        upd = [_dot(kdt[hh], v[rows, vslice(hh)]) for hh in range(RET_HEADS)]
        for hh in range(RET_HEADS):
            s_h = (sc[hh] * dintra_ref[hh]).astype(BF16)
            lhs = jnp.concatenate([s_h, qd[rows, qslice(hh)]], axis=1)
            rhs = jnp.concatenate([v[rows, vslice(hh)], state[hh].astype(BF16)], axis=0)
            oh = _dot(lhs, rhs)
            r_scr[:, vslice(hh)] = cdec_ref[:, vslice(hh)] * state[hh] + upd[hh]
            dev = oh - jnp.mean(oh, axis=-1, keepdims=True)
            oh = dev * lax.rsqrt(jnp.mean(dev * dev, axis=-1, keepdims=True) + EPS)
            oh = oh * g_ret_ref[:, vslice(hh)] + b_ret_ref[:, vslice(hh)]
            gh = gr[rows, vslice(hh)]
            gn_scr[rows, vslice(hh)] = (gh * _sigmoid(gh) * oh).astype(BF16)

    n_chunks = ts // RET_CHUNK

    hp = proj(OFF_HP, POOL_WIDTH)
    q = proj(OFF_Q, RET_QK_WIDTH)
    kt = proj(OFF_K, RET_QK_WIDTH).T

    t1 = (lax.broadcasted_iota(jnp.int32, (ts, POOL_GROUP_DIM), 0) + (s_idx * ts + 1)).astype(F32)
    pooled = []
    for gi, w in enumerate(POOL_WINDOWS):
        cols = slice(gi * POOL_GROUP_DIM, (gi + 1) * POOL_GROUP_DIM)
        cur = hp[:, cols]
        acc = jnp.concatenate([e_scr[:, cols], cur], axis=0)
        span = 1
        while span < w:
            acc = acc + pltpu.roll(acc, span, 0)
            span *= 2
        pooled.append(acc[POOL_HALO:, :] / jnp.minimum(t1, float(w)) - cur)
    e_scr[...] = hp[ts - POOL_HALO:, :]
    pooled = jnp.concatenate(pooled, axis=-1).astype(BF16)

    v = proj(OFF_V, RET_V_WIDTH).astype(BF16)

    q = rope(q)
    qd = jnp.concatenate([(q[c * RET_CHUNK:(c + 1) * RET_CHUNK, :] * qdec_ref[...]).astype(BF16)
                          for c in range(n_chunks)], axis=0)
    half = RET_QK_DIM // 2
    for hh in range(RET_HEADS):
        kh = kt[qslice(hh), :]
        swapped = jnp.concatenate([kh[half:, :], kh[:half, :]], axis=0)
        kh = (kh * cost_ref[...] + swapped * sint_ref[...]) * (RET_QK_DIM ** -0.5)
        for c in range(n_chunks):
            cols = slice(c * RET_CHUNK, (c + 1) * RET_CHUNK)
            kd_scr[qslice(hh), cols] = (kh[:, cols] * kdect_ref[qslice(hh), :]).astype(BF16)

    pair_w = 2 * POOL_GROUP_DIM
    y_pool = jnp.concatenate(
        [_dot(pooled[:, i * pair_w:(i + 1) * pair_w], w_pool_ref[i]) for i in range(POOL_WIDTH // pair_w)],
        axis=-1) * pool_scale_ref[...]
    qx = proj(OFF_QX, XA_WIDTH).astype(BF16)
    y_pool = _dot(y_pool.astype(BF16), w_a_ref[...])

    m_scr[...] = _sigmoid(proj(OFF_GATE, D_MODEL)) * y_pool

    probs, denom = [], []
    for hh in range(XA_HEADS):
        hs = qslice(hh)
        sc = _dot(qx[:, hs], kt_scr[hs, :]) * (XA_HEAD_DIM ** -0.5)
        p = jnp.exp(sc - jnp.max(sc, axis=-1, keepdims=True))
        probs.append(p.astype(BF16))
        denom.append(jnp.sum(p, axis=-1, keepdims=True))

    gr = proj(OFF_GR, RET_V_WIDTH)

    mem_parts = {}

    def attend():
        heads = [_dot(probs[hh], v_scr[:, qslice(hh)]) / denom[hh] for hh in range(XA_HEADS)]
        mem_parts["heads"] = jnp.concatenate(heads, axis=-1).astype(BF16)

    def gate_mem():
        mem_parts["gate"] = _sigmoid(proj(OFF_GATE + 2 * D_MODEL, D_MODEL))

    def add_mem():
        m_scr[...] += mem_parts["gate"] * _dot(mem_parts["heads"], w_c_ref[...])

    fills = [attend, gate_mem, add_mem]
    for c in range(n_chunks):
        retention_chunk(c, qd, v, gr)
        for fill in (fills[c:c + 1] if c < n_chunks - 1 else fills[c:]):
            fill()

    gate_ret = _sigmoid(proj(OFF_GATE + D_MODEL, D_MODEL))
    y_ret = _dot(gn_scr[...], w_r_ref[...])
    merged = (m_scr[...] + gate_ret * y_ret).astype(BF16)

    for r0 in range(0, ts, OUT_ROWS):
        rows = slice(r0, r0 + OUT_ROWS)
        o_ref[rows, :] = x_ref[rows, :] + _dot(merged[rows, :], w_out_ref[...])


def _retention_tables(seq):
    half = RET_QK_DIM // 2
    pos = np.arange(seq, dtype=np.float64)
    inv = ROPE_BASE ** (-np.arange(half, dtype=np.float64) / half)
    ang = pos[:, None] * inv[None, :]
    cos, sin = np.cos(ang), np.sin(ang)
    cos2 = np.concatenate([cos, cos], axis=-1)
    sin2 = np.concatenate([-sin, sin], axis=-1)
    c = RET_CHUNK
    log_gamma = np.log1p(-np.exp2(-5.0 - np.arange(RET_HEADS, dtype=np.float64)))
    lg = log_gamma[:, None, None]
    idx = np.arange(c, dtype=np.float64)
    rel = idx[:, None] - idx[None, :]
    q_decay = np.exp((idx + 1.0)[None, :, None] * lg)
    k_decay = np.exp((c - 1.0 - idx)[None, :, None] * lg)
    chunk_decay = np.exp(c * lg)
    decay_intra = np.where(rel >= 0, 1.0 / chunk_decay, 0.0)

    def lanes(a, width):
        r = a.shape[1]
        return np.broadcast_to(a.transpose(1, 0, 2), (r, RET_HEADS, width)).reshape(r, RET_HEADS * width)

    k_decay_t = np.broadcast_to(k_decay.transpose(0, 2, 1), (RET_HEADS, RET_QK_DIM, c))
    tables = (cos2, sin2, cos2.T, sin2.T, decay_intra, lanes(q_decay, RET_QK_DIM),
              k_decay_t.reshape(RET_QK_WIDTH, c), lanes(chunk_decay, RET_V_DIM))
    return tuple(jnp.asarray(np.ascontiguousarray(t), F32) for t in tables)


def _vmem_full():
    return pl.BlockSpec(memory_space=pltpu.VMEM)


def _cast_chunks(rows, n_steps):
    tiles = rows // BF16_ROWS
    assert rows % BF16_ROWS == 0
    return max(c for c in range(1, min(tiles, n_steps) + 1) if tiles % c == 0)


def _mixer(x, mem, tables, g_mix, w_in, w_pool_bd, pool_scale, w_a, g_ret, b_ret, w_r, g_mem,
           w_mem_kv, w_c, w_out, w_up, w_down):
    b, s, d = x.shape
    m = mem.shape[1]
    ts = min(MIXER_ROWS, s)
    assert s % ts == 0 and ts % RET_CHUNK == 0 and ts % OUT_ROWS == 0
    ns = s // ts
    up_chunks = _cast_chunks(w_up.shape[0], b * ns)
    down_chunks = _cast_chunks(w_down.shape[0], b * ns)

    def cast_spec(w, chunks):
        return pl.BlockSpec((w.shape[0] // chunks, w.shape[1]),
                            lambda bi, si: (jnp.minimum(bi * ns + si, chunks - 1), 0))

    cos2, sin2, cos2t, sin2t, dintra, qdec, kdect, cdec = tables
    row_tile = pl.BlockSpec((None, ts, d), lambda bi, si: (bi, si, 0))
    pos_tile = pl.BlockSpec((ts, RET_QK_DIM), lambda bi, si: (si, 0))
    pos_tile_t = pl.BlockSpec((RET_QK_DIM, ts), lambda bi, si: (0, si))
    in_specs = [row_tile,
                pl.BlockSpec((None, m, d), lambda bi, si: (bi, 0, 0)),
                pos_tile, pos_tile, pos_tile_t, pos_tile_t] + [_vmem_full()] * 16 + [
                cast_spec(w_up, up_chunks), cast_spec(w_down, down_chunks)]
    return pl.pallas_call(
        functools.partial(_mixer_kernel, up_chunks=up_chunks, down_chunks=down_chunks),
        grid=(b, ns),
        in_specs=in_specs,
        out_specs=[row_tile, cast_spec(w_up, up_chunks), cast_spec(w_down, down_chunks)],
        out_shape=[jax.ShapeDtypeStruct((b, s, d), F32),
                   jax.ShapeDtypeStruct(w_up.shape, BF16),
                   jax.ShapeDtypeStruct(w_down.shape, BF16)],
        scratch_shapes=[
            pltpu.VMEM((ts, d), BF16),
            pltpu.VMEM((POOL_HALO, POOL_WIDTH), F32),
            pltpu.VMEM((RET_QK_DIM, RET_V_WIDTH), F32),
            pltpu.VMEM((XA_WIDTH, m), BF16),
            pltpu.VMEM((m, XA_WIDTH), BF16),
            pltpu.VMEM((RET_QK_WIDTH, ts), BF16),
            pltpu.VMEM((ts, RET_V_WIDTH), BF16),
            pltpu.VMEM((ts, d), F32),
        ],
        compiler_params=pltpu.CompilerParams(
            dimension_semantics=("arbitrary", "arbitrary"),
            vmem_limit_bytes=V7X_VMEM_LIMIT_BYTES),
        name="mixer",
    )(x, mem, cos2, sin2, cos2t, sin2t, dintra, qdec, kdect, cdec,
      g_mix, w_in, w_pool_bd, pool_scale, w_a, g_ret, b_ret, w_r, g_mem, w_mem_kv, w_c, w_out,
      w_up, w_down)


def _ffn_kernel(x_ref, g_ffn_ref, w_up_ref, conv_w_ref, conv_b_ref, w_down_ref, g_out_ref,
                o_ref, h_scr, a_scr, carry_scr, g_scr, *, final_norm):
    ts = x_ref.shape[0]

    @pl.when(pl.program_id(1) == 0)
    def _start_of_sequence():
        carry_scr[...] = jnp.zeros(carry_scr.shape, F32)

    h_scr[...] = (_rms(x_ref[...]) * g_ffn_ref[...]).astype(BF16)
    for j in range(FFN_HIDDEN // FFN_COLS):
        cs = slice(j * FFN_COLS, (j + 1) * FFN_COLS)
        a = _dot(h_scr[...], w_up_ref[:, cs])
        gate = _dot(h_scr[...], w_up_ref[:, FFN_HIDDEN + j * FFN_COLS:FFN_HIDDEN + (j + 1) * FFN_COLS])
        buf = a_scr.at[j % 2]
        buf[0:CONV_HALO, :] = carry_scr[:, cs]
        buf[CONV_HALO:CONV_HALO + ts, :] = a
        carry_scr[:, cs] = a[ts - CONV_HALO:, :]
        conv = a * conv_w_ref[CONV_WIDTH - 1:CONV_WIDTH, cs] + conv_b_ref[:, cs]
        for tap in range(CONV_WIDTH - 1):
            back = CONV_WIDTH - 1 - tap
            conv = conv + buf[CONV_HALO - back:CONV_HALO - back + ts, :] * conv_w_ref[tap:tap + 1, cs]
        g_scr[:, cs] = (jax.nn.gelu(conv) * gate).astype(BF16)
    for r0 in range(0, ts, OUT_ROWS):
        rows = slice(r0, r0 + OUT_ROWS)
        y = x_ref[rows, :] + _dot(g_scr[rows, :], w_down_ref[...])
        if final_norm:
            y = _rms(y) * g_out_ref[...]
        o_ref[rows, :] = y


def _ffn(x, g_ffn, w_up, conv_w, conv_b, w_down, g_out, final_norm):
    b, s, d = x.shape
    ts = min(FFN_ROWS, s)
    assert s % ts == 0 and FFN_HIDDEN % FFN_COLS == 0 and ts % OUT_ROWS == 0
    row_tile = pl.BlockSpec((None, ts, d), lambda bi, si: (bi, si, 0))
    return pl.pallas_call(
        functools.partial(_ffn_kernel, final_norm=final_norm),
        grid=(b, s // ts),
        in_specs=[row_tile] + [_vmem_full()] * 6,
        out_specs=row_tile,
        out_shape=jax.ShapeDtypeStruct((b, s, d), F32),
        scratch_shapes=[
            pltpu.VMEM((ts, d), BF16),
            pltpu.VMEM((2, ts + CONV_HALO, FFN_COLS), F32),
            pltpu.VMEM((CONV_HALO, FFN_HIDDEN), F32),
            pltpu.VMEM((ts, FFN_HIDDEN), BF16),
        ],
        compiler_params=pltpu.CompilerParams(
            dimension_semantics=("arbitrary", "arbitrary"),
            vmem_limit_bytes=V7X_VMEM_LIMIT_BYTES),
        name="ffn",
    )(x, g_ffn, w_up, conv_w, conv_b, w_down, g_out)


def _pair_block_diag(w):
    g, c, dd = w.shape
    zero = jnp.zeros((g // 2, c, dd), w.dtype)
    top = jnp.concatenate([w[0::2], zero], axis=2)
    bottom = jnp.concatenate([zero, w[1::2]], axis=2)
    return jnp.concatenate([top, bottom], axis=1)


def kernel(x, mem, g_mix, w_in, w_pool, pool_scale, w_a, g_ret, b_ret, w_r, g_mem, w_mem_kv, w_c,
           w_out, g_ffn, w_up, conv_w, conv_b, w_down, g_final):
    depth = w_in.shape[0]
    tables = _retention_tables(x.shape[1])
    row = lambda a: a.reshape(1, -1)
    for l in range(depth):
        x, w_up_l, w_down_l = _mixer(
            x, mem, tables, row(g_mix[l]), w_in[l].astype(BF16),
            _pair_block_diag(w_pool[l]).astype(BF16), row(pool_scale[l]), w_a[l].astype(BF16),
            row(g_ret[l]), row(b_ret[l]), w_r[l].astype(BF16), row(g_mem[l]),
            w_mem_kv[l].astype(BF16), w_c[l].astype(BF16), w_out[l].astype(BF16), w_up[l], w_down[l])
        x = _ffn(x, row(g_ffn[l]), w_up_l, conv_w[l], row(conv_b[l]), w_down_l, row(g_final),
                 final_norm=(l == depth - 1))
    return x
```
